```python
import math
import jax, jax.numpy as jnp
from jax import lax
import numpy as np

D_MODEL = 1024
BATCH = 16
SEQ = 4096
DEPTH = 4

CHUNK = 64
Q_BLOCK = 128
N_MIXERS = 2
N_FOX = (DEPTH + 1) // 2
N_GDN = DEPTH // 2
FOX_HEADS = 16
FOX_HEAD_DIM = D_MODEL // FOX_HEADS
GDN_HEADS = 8
GDN_HEAD_DIM = D_MODEL // GDN_HEADS
CONV_WIDTH = 4
N_EXPERTS = 16
N_GROUPS = 4
EXPERTS_PER_GROUP = N_EXPERTS // N_GROUPS
TOPK_GROUPS = 1
TOP_K = 2
D_EXPERT = D_MODEL // 2
MOE_BLOCK = 256
DN_ALPHA = (2.0 * DEPTH) ** 0.25
DN_BETA = (8.0 * DEPTH) ** -0.25
LN_EPS = 1e-5
RMS_EPS = 1e-6
L2_EPS = 1e-6

kernel_name = 'hybrid_fox_gdn_groupmoe_deepnorm'


def _layer_norm(x, g, b):
    xf = x.astype(jnp.float32)
    mu = jnp.mean(xf, axis=-1, keepdims=True)
    xc = xf - mu
    var = jnp.mean(xc * xc, axis=-1, keepdims=True)
    y = xc * lax.rsqrt(var + LN_EPS) * g.astype(jnp.float32) + b.astype(jnp.float32)
    return y.astype(x.dtype)


def _rms_norm(x, g):
    xf = x.astype(jnp.float32)
    y = xf * lax.rsqrt(jnp.mean(xf * xf, axis=-1, keepdims=True) + RMS_EPS)
    return (y * g.astype(jnp.float32)).astype(x.dtype)


def _l2norm(x):
    xf = x.astype(jnp.float32)
    return xf * lax.rsqrt(jnp.sum(xf * xf, axis=-1, keepdims=True) + L2_EPS)


def fox_mixer(x, w_in, b_f, q_gain, k_gain, w_out):
    bsz, seq, d = x.shape
    h, dh = FOX_HEADS, FOX_HEAD_DIM
    proj = x @ w_in
    q, k, v, og, f_logit = jnp.split(proj, [d, 2 * d, 3 * d, 4 * d], axis=-1)
    q = _rms_norm(q.reshape(bsz, seq, h, dh), q_gain)
    k = _rms_norm(k.reshape(bsz, seq, h, dh), k_gain)
    v = v.reshape(bsz, seq, h, dh)
    log_f = jax.nn.log_sigmoid(f_logit.astype(jnp.float32) + b_f.astype(jnp.float32))
    c = jnp.cumsum(log_f, axis=1).transpose(0, 2, 1)
    scale = dh ** -0.5
    outs = []
    for blk in range(seq // Q_BLOCK):
        lo, hi = blk * Q_BLOCK, (blk + 1) * Q_BLOCK
        s = jnp.einsum('bqhd,bkhd->bhqk', q[:, lo:hi], k[:, :hi],
                       preferred_element_type=jnp.float32) * scale
        s = s + c[:, :, lo:hi, None] - c[:, :, None, :hi]
        qpos = lo + jnp.arange(Q_BLOCK)[:, None]
        kpos = jnp.arange(hi)[None, :]
        s = jnp.where(kpos <= qpos, s, -jnp.inf)
        p = jax.nn.softmax(s, axis=-1).astype(v.dtype)
        outs.append(jnp.einsum('bhqk,bkhd->bqhd', p, v[:, :hi]))
    o = jnp.concatenate(outs, axis=1).reshape(bsz, seq, d)
    return (o * jax.nn.sigmoid(og)) @ w_out


def _causal_conv(x, w):
    kw, ch = w.shape
    return lax.conv_general_dilated(
        x, w.reshape(kw, 1, ch).astype(x.dtype), window_strides=(1,), padding=[(kw - 1, 0)],
        dimension_numbers=('NWC', 'WIO', 'NWC'), feature_group_count=ch)


def _chunk_gated_delta_rule(q, k, v, beta, g):
    bsz, seq, h, dk = q.shape
    dv = v.shape[-1]
    n = seq // CHUNK

    def to_chunks(t):
        t = t.astype(jnp.float32).reshape((bsz, n, CHUNK, h) + t.shape[3:])
        return jnp.moveaxis(t, 3, 1)

    q, k, v, beta, g = (to_chunks(t) for t in (q, k, v, beta, g))
    g = jnp.cumsum(g, axis=-1)
    idx = jnp.arange(CHUNK)
    strict = idx[:, None] > idx[None, :]
    incl = idx[:, None] >= idx[None, :]
    decay = jnp.exp(jnp.where(incl, g[..., :, None] - g[..., None, :], -jnp.inf))
    kb = k * beta[..., None]
    m = jnp.where(strict, jnp.einsum('bhncd,bhnsd->bhncs', kb, k) * decay, 0.0)
    sys_mat = jnp.eye(CHUNK, dtype=jnp.float32) + m
    rhs = jnp.concatenate([v * beta[..., None], kb * jnp.exp(g)[..., None]], axis=-1)
    sol = lax.linalg.triangular_solve(sys_mat, rhs, left_side=True, lower=True, unit_diagonal=True)
    u, w = sol[..., :dv], sol[..., dv:]
    attn = jnp.einsum('bhncd,bhnsd->bhncs', q, k) * decay
    q_dec = q * jnp.exp(g)[..., None]
    k_dec = k * jnp.exp(g[..., -1:] - g)[..., None]
    g_last = jnp.exp(g[..., -1])

    def step(state, xs):
        q_i, k_i, u_i, w_i, a_i, gl_i = xs
        v_new = u_i - jnp.einsum('bhcd,bhde->bhce', w_i, state)
        o_i = jnp.einsum('bhcd,bhde->bhce', q_i, state) + jnp.einsum('bhcs,bhse->bhce', a_i, v_new)
        state = state * gl_i[..., None, None] + jnp.einsum('bhcd,bhce->bhde', k_i, v_new)
        return state, o_i

    xs = tuple(jnp.moveaxis(t, 2, 0) for t in (q_dec, k_dec, u, w, attn, g_last))
    state0 = jnp.zeros((bsz, h, dk, dv), jnp.float32)
    _, o = lax.scan(step, state0, xs)
    return o.transpose(1, 0, 3, 2, 4).reshape(bsz, seq, h, dv)


def gdn_mixer(x, w_in, conv_w, a_log, dt_bias, norm_g, w_out):
    bsz, seq, d = x.shape
    h, dk = GDN_HEADS, GDN_HEAD_DIM
    proj = x @ w_in
    qkv, og, b_logit, a_logit = jnp.split(proj, [3 * d, 4 * d, 4 * d + h], axis=-1)
    qkv = jax.nn.silu(_causal_conv(qkv, conv_w))
    q, k, v = jnp.split(qkv, 3, axis=-1)
    q = _l2norm(q.reshape(bsz, seq, h, dk)) * (dk ** -0.5)
    k = _l2norm(k.reshape(bsz, seq, h, dk))
    v = v.reshape(bsz, seq, h, dk)
    beta = jax.nn.sigmoid(b_logit.astype(jnp.float32))
    g = -jnp.exp(a_log.astype(jnp.float32)) * jax.nn.softplus(
        a_logit.astype(jnp.float32) + dt_bias.astype(jnp.float32))
    o = _chunk_gated_delta_rule(q, k, v, beta, g).astype(x.dtype)
    o = _rms_norm(o, norm_g) * jax.nn.silu(og.reshape(bsz, seq, h, dk))
    return o.reshape(bsz, seq, d) @ w_out


def _route(hf, router_w, router_b):
    ntok = hf.shape[0]
    probs = jax.nn.softmax((hf @ router_w).astype(jnp.float32), axis=-1)
    sel = (probs + router_b.astype(jnp.float32)).reshape(ntok, N_GROUPS, EXPERTS_PER_GROUP)
    grp_score = jnp.sum(lax.top_k(sel, 2)[0], axis=-1)
    _, gidx = lax.top_k(grp_score, TOPK_GROUPS)
    gmask = jnp.sum(jax.nn.one_hot(gidx, N_GROUPS, dtype=jnp.float32), axis=1) > 0
    masked = jnp.where(gmask[:, :, None], sel, -jnp.inf).reshape(ntok, N_EXPERTS)
    _, eidx = lax.top_k(masked, TOP_K)
    gate = jnp.take_along_axis(probs, eidx, axis=-1)
    gate = gate / jnp.sum(gate, axis=-1, keepdims=True)
    return eidx, gate


def moe_ffn(h, router_w, router_b, w_gate, w_up, w_down):
    bsz, seq, d = h.shape
    hf = h.reshape(-1, d)
    ntok = hf.shape[0]
    eidx, gate = _route(hf, router_w, router_b)
    n_assign = ntok * TOP_K
    e_flat = eidx.reshape(-1)
    tok_flat = jnp.repeat(jnp.arange(ntok, dtype=jnp.int32), TOP_K)
    g_flat = gate.reshape(-1)
    order = jnp.argsort(e_flat)
    e_s, tok_s, g_s = e_flat[order], tok_flat[order], g_flat[order]
    counts = jnp.bincount(e_flat, length=N_EXPERTS)
    start = jnp.cumsum(counts) - counts
    padded = (counts + MOE_BLOCK - 1) // MOE_BLOCK * MOE_BLOCK
    pend = jnp.cumsum(padded)
    pstart = pend - padded
    slot = pstart[e_s] + (jnp.arange(n_assign) - start[e_s])
    n_blocks = -(-(n_assign + N_EXPERTS * (MOE_BLOCK - 1)) // MOE_BLOCK)
    n_slots = n_blocks * MOE_BLOCK
    slot_tok = jnp.zeros((n_slots,), jnp.int32).at[slot].set(tok_s)
    slot_gate = jnp.zeros((n_slots,), jnp.float32).at[slot].set(g_s)
    block_expert = jnp.minimum(
        jnp.searchsorted(pend, jnp.arange(n_blocks) * MOE_BLOCK, side='right'), N_EXPERTS - 1)

    def expert_block(args):
        e, toks, gts = args
        xb = hf[toks]
        hid = jax.nn.silu(xb @ w_gate[e]) * (xb @ w_up[e])
        return ((hid @ w_down[e]).astype(jnp.float32) * gts[:, None])

    yb = lax.map(expert_block, (block_expert, slot_tok.reshape(n_blocks, MOE_BLOCK),
                                slot_gate.reshape(n_blocks, MOE_BLOCK)))
    out = jax.ops.segment_sum(yb.reshape(n_slots, d), slot_tok, num_segments=ntok)
    return out.astype(h.dtype).reshape(bsz, seq, d)


def setup_inputs(seed: int = 0) -> dict:
    key = jax.random.key(seed)
    ks = iter(jax.random.split(key, 32))
    d, e, f = D_MODEL, N_EXPERTS, D_EXPERT
    nrm = lambda shape, s: jax.random.normal(next(ks), shape, jnp.float32) * s
    ones_n = lambda shape: 1.0 + 0.02 * jax.random.normal(next(ks), shape, jnp.float32)
    x = jax.random.normal(next(ks), (BATCH, SEQ, d), jnp.float32)
    ln_mix_g = ones_n((DEPTH, d))
    ln_mix_b = nrm((DEPTH, d), 0.02)
    ln_ffn_g = ones_n((DEPTH, d))
    ln_ffn_b = nrm((DEPTH, d), 0.02)
    router_w = nrm((d, e), d ** -0.5)
    router_b = nrm((e,), 0.01)
    fox_w_in = nrm((N_FOX, d, 4 * d + FOX_HEADS), d ** -0.5)
    fox_b_f = jax.random.uniform(next(ks), (N_FOX, FOX_HEADS), jnp.float32, 1.0, 5.0)
    fox_q_gain = ones_n((N_FOX, FOX_HEAD_DIM))
    fox_k_gain = ones_n((N_FOX, FOX_HEAD_DIM))
    fox_w_out = nrm((N_FOX, d, d), d ** -0.5 * DN_BETA)
    gdn_w_in = nrm((N_GDN, d, 4 * d + 2 * GDN_HEADS), d ** -0.5)
    gdn_conv_w = nrm((N_GDN, CONV_WIDTH, 3 * d), CONV_WIDTH ** -0.5)
    gdn_a_log = jnp.log(jax.random.uniform(next(ks), (N_GDN, GDN_HEADS), jnp.float32, 1.0, 16.0))
    dt = jnp.exp(jax.random.uniform(next(ks), (N_GDN, GDN_HEADS), jnp.float32,
                                    math.log(1e-3), math.log(1e-1)))
    gdn_dt_bias = dt + jnp.log(-jnp.expm1(-dt))
    gdn_norm_g = ones_n((N_GDN, GDN_HEAD_DIM))
    gdn_w_out = nrm((N_GDN, d, d), d ** -0.5 * DN_BETA)
    moe_w_gate = nrm((DEPTH, e, d, f), d ** -0.5)
    moe_w_up = nrm((DEPTH, e, d, f), d ** -0.5)
    moe_w_down = nrm((DEPTH, e, f, d), f ** -0.5 * DN_BETA)
    return {'x': x, 'ln_mix_g': ln_mix_g, 'ln_mix_b': ln_mix_b, 'ln_ffn_g': ln_ffn_g,
            'ln_ffn_b': ln_ffn_b, 'router_w': router_w, 'router_b': router_b,
            'fox_w_in': fox_w_in, 'fox_b_f': fox_b_f, 'fox_q_gain': fox_q_gain,
            'fox_k_gain': fox_k_gain, 'fox_w_out': fox_w_out, 'gdn_w_in': gdn_w_in,
            'gdn_conv_w': gdn_conv_w, 'gdn_a_log': gdn_a_log, 'gdn_dt_bias': gdn_dt_bias,
            'gdn_norm_g': gdn_norm_g, 'gdn_w_out': gdn_w_out, 'moe_w_gate': moe_w_gate,
            'moe_w_up': moe_w_up, 'moe_w_down': moe_w_down}


def reference(x, ln_mix_g, ln_mix_b, ln_ffn_g, ln_ffn_b, router_w, router_b,
              fox_w_in, fox_b_f, fox_q_gain, fox_k_gain, fox_w_out,
              gdn_w_in, gdn_conv_w, gdn_a_log, gdn_dt_bias, gdn_norm_g, gdn_w_out,
              moe_w_gate, moe_w_up, moe_w_down):
    h = x
    for i in range(DEPTH):
        j = i // N_MIXERS
        if i % N_MIXERS == 0:
            mix = fox_mixer(h, fox_w_in[j], fox_b_f[j], fox_q_gain[j], fox_k_gain[j], fox_w_out[j])
        else:
            mix = gdn_mixer(h, gdn_w_in[j], gdn_conv_w[j], gdn_a_log[j], gdn_dt_bias[j],
                            gdn_norm_g[j], gdn_w_out[j])
        h = _layer_norm(DN_ALPHA * h + mix, ln_mix_g[i], ln_mix_b[i])
        ffn = moe_ffn(h, router_w, router_b, moe_w_gate[i], moe_w_up[i], moe_w_down[i])
        h = _layer_norm(DN_ALPHA * h + ffn, ln_ffn_g[i], ln_ffn_b[i])
    return h
```

```python
import functools
import math

import jax
import jax.numpy as jnp
from jax import lax
from jax.experimental import pallas as pl
from jax.experimental.pallas import tpu as pltpu

F32 = jnp.float32
BF16 = jnp.bfloat16

N_GROUPS = 4
EXPERTS_PER_GROUP = 4
PAIRS_PER_GROUP = 6
CHUNK = 64
CONV_WIDTH = 4
LN_EPS = 1e-5
RMS_EPS = 1e-6
L2_EPS = 1e-6

LANES = 128
VMEM_LIMIT = 56 * 1024 * 1024

ROW_TILE = 512
ATT_TILE = 512
GDN_TILE = 256
MOE_BLOCK = 256


def _cparams(sem):
    return pltpu.CompilerParams(dimension_semantics=sem, vmem_limit_bytes=VMEM_LIMIT)


def _layer_norm_rows(y, g, b):
    mu = jnp.mean(y, axis=-1, keepdims=True)
    yc = y - mu
    var = jnp.mean(yc * yc, axis=-1, keepdims=True)
    return yc * lax.rsqrt(var + LN_EPS) * g + b


def _sigmoid(x):
    return 1.0 / (1.0 + jnp.exp(-x))


def _silu(x):
    return x * _sigmoid(x)


def _softplus(x):
    return jnp.maximum(x, 0.0) + jnp.log(1.0 + jnp.exp(-jnp.abs(x)))


def _inproj_kernel(x_ref, w_ref, ws_ref, o_ref, os_ref, *, col_chunk):
    xb = x_ref[...].astype(BF16)
    width = o_ref.shape[1]
    for j in range(width // col_chunk):
        sl = slice(j * col_chunk, (j + 1) * col_chunk)
        o_ref[:, sl] = jnp.dot(xb, w_ref[:, sl], preferred_element_type=F32).astype(BF16)
    os_ref[...] = jnp.dot(xb, ws_ref[...], preferred_element_type=F32)


def _inproj(x, w_main, w_side):
    n, d = x.shape
    width = w_main.shape[1]
    tm = min(ROW_TILE, n)
    return pl.pallas_call(
        functools.partial(_inproj_kernel, col_chunk=1024),
        grid=(n // tm,),
        in_specs=[
            pl.BlockSpec((tm, d), lambda i: (i, 0)),
            pl.BlockSpec((d, width), lambda i: (0, 0)),
            pl.BlockSpec((d, LANES), lambda i: (0, 0)),
        ],
        out_specs=[
            pl.BlockSpec((tm, width), lambda i: (i, 0)),
            pl.BlockSpec((tm, LANES), lambda i: (i, 0)),
        ],
        out_shape=[
            jax.ShapeDtypeStruct((n, width), BF16),
            jax.ShapeDtypeStruct((n, LANES), F32),
        ],
        compiler_params=_cparams(("parallel",)),
        name="inproj",
    )(x, w_main, w_side)


def _fox_prep_kernel(p_ref, gain_ref, o_ref, *, head_dim):
    lane = lax.broadcasted_iota(jnp.int32, (1, LANES), 1)
    first = lane < head_dim
    for j in range(o_ref.shape[1] // LANES):
        sl = slice(j * LANES, (j + 1) * LANES)
        x = p_ref[:, sl].astype(F32)
        xx = x * x
        s_a = jnp.sum(jnp.where(first, xx, 0.0), axis=-1, keepdims=True)
        s_b = jnp.sum(jnp.where(first, 0.0, xx), axis=-1, keepdims=True)
        r = jnp.where(first, lax.rsqrt(s_a / head_dim + RMS_EPS),
                      lax.rsqrt(s_b / head_dim + RMS_EPS))
        o_ref[:, sl] = (x * r * gain_ref[:, sl]).astype(BF16)


def _fox_prep(proj, gains, head_dim):
    n = proj.shape[0]
    width = gains.shape[1]
    tm = min(ROW_TILE, n)
    return pl.pallas_call(
        functools.partial(_fox_prep_kernel, head_dim=head_dim),
        grid=(n // tm,),
        in_specs=[
            pl.BlockSpec((tm, width), lambda i: (i, 0)),
            pl.BlockSpec((1, width), lambda i: (0, 0)),
        ],
        out_specs=pl.BlockSpec((tm, width), lambda i: (i, 0)),
        out_shape=jax.ShapeDtypeStruct((n, width), BF16),
        compiler_params=_cparams(("parallel",)),
        name="fox_prep",
    )(proj, gains)


def _prefix_sum_lanes(x, period):
    pos = lax.broadcasted_iota(jnp.int32, x.shape, 1) % period
    d = 1
    while d < period:
        x = x + jnp.where(pos >= d, pltpu.roll(x, d, axis=1), 0.0)
        d *= 2
    return x


def _fox_gate_kernel(f_ref, b_ref, c_ref):
    z = f_ref[0] + b_ref[...]
    log_f = -_softplus(-z)
    c_ref[0] = _prefix_sum_lanes(log_f, z.shape[1])


def _fox_gates(f_logit_t, b_f):
    bsz, h, s = f_logit_t.shape
    return pl.pallas_call(
        _fox_gate_kernel,
        grid=(bsz,),
        in_specs=[pl.BlockSpec((1, h, s), lambda b: (b, 0, 0)),
                  pl.BlockSpec((h, 1), lambda b: (0, 0))],
        out_specs=pl.BlockSpec((1, h, s), lambda b: (b, 0, 0)),
        out_shape=jax.ShapeDtypeStruct((bsz, h, s), F32),
        compiler_params=_cparams(("parallel",)),
        name="fox_gates",
    )(f_logit_t, b_f.reshape(h, 1))


def _gdn_gate_kernel(b_ref, a_ref, alog_ref, dt_ref, beta_ref, gc_ref):
    beta_ref[0] = _sigmoid(b_ref[0])
    g = -jnp.exp(alog_ref[...]) * _softplus(a_ref[0] + dt_ref[...])
    gc_ref[0] = _prefix_sum_lanes(g, CHUNK)


def _gdn_gates(b_logit_t, a_logit_t, a_log, dt_bias):
    bsz, h, s = b_logit_t.shape
    spec = pl.BlockSpec((1, h, s), lambda b: (b, 0, 0))
    vec = pl.BlockSpec((h, 1), lambda b: (0, 0))
    return pl.pallas_call(
        _gdn_gate_kernel,
        grid=(bsz,),
        in_specs=[spec, spec, vec, vec],
        out_specs=[spec, spec],
        out_shape=[jax.ShapeDtypeStruct((bsz, h, s), F32)] * 2,
        compiler_params=_cparams(("parallel",)),
        name="gdn_gates",
    )(b_logit_t, a_logit_t, a_log.reshape(h, 1), dt_bias.reshape(h, 1))


def _fox_attn_kernel(q_ref, k_ref, v_ref, ccol_ref, crow_ref, o_ref,
                     m_ref, l_ref, acc_ref, *, head_dim, blk):
    qi = pl.program_id(2)
    lane = lax.broadcasted_iota(jnp.int32, (1, LANES), 1)
    first = lane < head_dim
    q = q_ref[0]
    zero = jnp.zeros_like(q)
    q_heads = (jnp.where(first, q, zero), jnp.where(first, zero, q))
    c_q = (ccol_ref[0, 0, :, 0:1], ccol_ref[0, 0, :, 1:2])

    m_ref[...] = jnp.full(m_ref.shape, -1e30, F32)
    l_ref[...] = jnp.zeros(l_ref.shape, F32)
    acc_ref[...] = jnp.zeros(acc_ref.shape, F32)

    def step(j, masked):
        start = pl.multiple_of(j * blk, blk)
        k = k_ref[0, pl.ds(start, blk), :]
        v = v_ref[0, pl.ds(start, blk), :]
        c_k = crow_ref[0, 0, :, pl.ds(start, blk)]
        alphas, pvs = [], []
        for hh in range(2):
            s0 = lax.dot_general(q_heads[hh], k, (((1,), (1,)), ((), ())),
                                 preferred_element_type=F32)
            s0 = s0 - c_k[hh:hh + 1, :]
            if masked:
                row = lax.broadcasted_iota(jnp.int32, s0.shape, 0)
                col = lax.broadcasted_iota(jnp.int32, s0.shape, 1)
                s0 = jnp.where(col <= row, s0, -jnp.inf)
            m_old = m_ref[hh]
            m_new = jnp.maximum(m_old, jnp.max(s0, axis=-1, keepdims=True) + c_q[hh])
            p = jnp.exp(s0 + (c_q[hh] - m_new))
            alpha = jnp.exp(m_old - m_new)
            l_ref[hh] = alpha * l_ref[hh] + jnp.sum(p, axis=-1, keepdims=True)
            m_ref[hh] = m_new
            pvs.append(jnp.dot(p.astype(BF16), v, preferred_element_type=F32))
            alphas.append(alpha)
        acc_ref[...] = (acc_ref[...] * jnp.where(first, alphas[0], alphas[1])
                        + jnp.where(first, pvs[0], pvs[1]))

    def body(j, carry):
        step(j, False)
        return carry

    lax.fori_loop(0, qi, body, 0)
    step(qi, True)
    o_ref[0] = (acc_ref[...] / jnp.where(first, l_ref[0], l_ref[1])).astype(BF16)


def _fox_attention(qk, proj, c_col, c_row, bsz, seq, n_heads, head_dim):
    d = n_heads * head_dim
    pairs = d // LANES
    blk = min(ATT_TILE, seq)
    qk3 = qk.reshape(bsz, seq, 2 * d)
    proj3 = proj.reshape(bsz, seq, proj.shape[1])
    return pl.pallas_call(
        functools.partial(_fox_attn_kernel, head_dim=head_dim, blk=blk),
        grid=(bsz, pairs, seq // blk),
        in_specs=[
            pl.BlockSpec((1, blk, LANES), lambda b, p, i: (b, i, p)),
            pl.BlockSpec((1, seq, LANES), lambda b, p, i: (b, 0, pairs + p)),
            pl.BlockSpec((1, seq, LANES), lambda b, p, i: (b, 0, 2 * pairs + p)),
            pl.BlockSpec((1, 1, blk, 2), lambda b, p, i: (b, p, i, 0)),
            pl.BlockSpec((1, 1, 2, seq), lambda b, p, i: (b, p, 0, 0)),
        ],
        out_specs=pl.BlockSpec((1, blk, LANES), lambda b, p, i: (b, i, p)),
        out_shape=jax.ShapeDtypeStruct((bsz, seq, d), BF16),
        scratch_shapes=[
            pltpu.VMEM((2, blk, 1), F32),
            pltpu.VMEM((2, blk, 1), F32),
            pltpu.VMEM((blk, LANES), F32),
        ],
        compiler_params=_cparams(("parallel", "parallel", "arbitrary")),
        name="fox_attention",
    )(qk3, qk3, proj3, c_col, c_row)


def _gdn_prep_kernel(p_ref, w_ref, o_ref, buf_ref, *, d_model, head_dim, q_scale):
    ts = p_ref.shape[1]
    pad = 8

    @pl.when(pl.program_id(1) == 0)
    def _():
        buf_ref[0:pad, :] = jnp.zeros((pad, buf_ref.shape[1]), F32)

    buf_ref[pad:pad + ts, :] = p_ref[0].astype(F32)
    for j in range(o_ref.shape[2] // LANES):
        sl = slice(j * LANES, (j + 1) * LANES)
        acc = buf_ref[pad:pad + ts, sl] * w_ref[CONV_WIDTH - 1:CONV_WIDTH, sl]
        for dlt in range(1, CONV_WIDTH):
            acc = acc + (buf_ref[pad - dlt:pad - dlt + ts, sl]
                         * w_ref[CONV_WIDTH - 1 - dlt:CONV_WIDTH - dlt, sl])
        y = _silu(acc)
        if j * LANES < 2 * d_model:
            y = y * lax.rsqrt(jnp.sum(y * y, axis=-1, keepdims=True) + L2_EPS)
            if j * LANES < d_model:
                y = y * q_scale
        o_ref[0, :, sl] = y.astype(BF16)
    buf_ref[0:pad, :] = buf_ref[ts:ts + pad, :]


def _gdn_prep(proj3, conv_w, d_model, head_dim):
    bsz, seq, _ = proj3.shape
    width = 3 * d_model
    ts = min(ROW_TILE, seq)
    return pl.pallas_call(
        functools.partial(_gdn_prep_kernel, d_model=d_model, head_dim=head_dim,
                          q_scale=head_dim ** -0.5),
        grid=(bsz, seq // ts),
        in_specs=[
            pl.BlockSpec((1, ts, width), lambda b, i: (b, i, 0)),
            pl.BlockSpec((CONV_WIDTH, width), lambda b, i: (0, 0)),
        ],
        out_specs=pl.BlockSpec((1, ts, width), lambda b, i: (b, i, 0)),
        out_shape=jax.ShapeDtypeStruct((bsz, seq, width), BF16),
        scratch_shapes=[pltpu.VMEM((ts + 8, width), F32)],
        compiler_params=_cparams(("parallel", "arbitrary")),
        name="gdn_prep",
    )(proj3, conv_w)


def _unit_lower_inverse(m_strict):
    c = m_strict.shape[0]
    row = lax.broadcasted_iota(jnp.int32, (c, c), 0)
    col = lax.broadcasted_iota(jnp.int32, (c, c), 1)
    t = jnp.where(row == col, 1.0, 0.0) - m_strict
    p = m_strict
    k = 2
    while k < c:
        pb = p.astype(BF16)
        p = jnp.dot(pb, pb, preferred_element_type=F32)
        t = t + jnp.dot(t.astype(BF16), p.astype(BF16), preferred_element_type=F32)
        k *= 2
    return t


def _gdn_chunk_kernel(q_ref, k_ref, v_ref, col_ref, row_ref, o_ref, state_ref,
                      *, n_heads, head_dim):
    @pl.when(pl.program_id(1) == 0)
    def _():
        state_ref[...] = jnp.zeros(state_ref.shape, F32)

    n_chunks = q_ref.shape[1] // CHUNK
    ri = lax.broadcasted_iota(jnp.int32, (CHUNK, CHUNK), 0)
    ci = lax.broadcasted_iota(jnp.int32, (CHUNK, CHUNK), 1)

    def chunk(c, carry):
        start = pl.multiple_of(c * CHUNK, CHUNK)
        cols = col_ref[0, c]
        rows = row_ref[0, c]
        for h in range(n_heads):
            sl = slice(h * head_dim, (h + 1) * head_dim)
            q = q_ref[0, pl.ds(start, CHUNK), sl]
            k = k_ref[0, pl.ds(start, CHUNK), sl]
            v = v_ref[0, pl.ds(start, CHUNK), sl]
            beta = cols[:, h:h + 1]
            g_col = cols[:, n_heads + h:n_heads + h + 1]
            g_row = rows[n_heads + h:n_heads + h + 1, :]
            g_last = g_col[CHUNK - 1:CHUNK, :]
            decay = jnp.exp(jnp.where(ri >= ci, g_col - g_row, -jnp.inf))
            kf = k.astype(F32)
            kb = kf * beta
            kk = lax.dot_general(kb.astype(BF16), k, (((1,), (1,)), ((), ())),
                                 preferred_element_type=F32)
            t_inv = _unit_lower_inverse(jnp.where(ri > ci, kk * decay, 0.0))
            e_g = jnp.exp(g_col)
            rhs = jnp.concatenate([v.astype(F32) * beta, kb * e_g], axis=1).astype(BF16)
            sol = jnp.dot(t_inv.astype(BF16), rhs, preferred_element_type=F32)
            u = sol[:, :head_dim]
            w = sol[:, head_dim:]
            attn = lax.dot_general(q, k, (((1,), (1,)), ((), ())),
                                   preferred_element_type=F32) * decay
            q_dec = (q.astype(F32) * e_g).astype(BF16)
            k_dec = kf * jnp.exp(g_last - g_col)
            state = state_ref[h]
            sb = state.astype(BF16)
            v_new = u - jnp.dot(w.astype(BF16), sb, preferred_element_type=F32)
            vb = v_new.astype(BF16)
            o = (jnp.dot(q_dec, sb, preferred_element_type=F32)
                 + jnp.dot(attn.astype(BF16), vb, preferred_element_type=F32))
            o_ref[0, pl.ds(start, CHUNK), sl] = o.astype(BF16)
            state_ref[h] = state * jnp.exp(g_last) + jnp.dot(
                k_dec.T.astype(BF16), vb, preferred_element_type=F32)
        return carry

    lax.fori_loop(0, n_chunks, chunk, 0)


def _gdn_delta_rule(qkv3, cols, rows, n_heads, head_dim):
    bsz, seq, _ = qkv3.shape
    d = n_heads * head_dim
    ts = min(GDN_TILE, seq)
    cpt = ts // CHUNK
    return pl.pallas_call(
        functools.partial(_gdn_chunk_kernel, n_heads=n_heads, head_dim=head_dim),
        grid=(bsz, seq // ts),
        in_specs=[
            pl.BlockSpec((1, ts, d), lambda b, i: (b, i, 0)),
            pl.BlockSpec((1, ts, d), lambda b, i: (b, i, 1)),
            pl.BlockSpec((1, ts, d), lambda b, i: (b, i, 2)),
            pl.BlockSpec((1, cpt, CHUNK, 2 * n_heads), lambda b, i: (b, i, 0, 0)),
            pl.BlockSpec((1, cpt, 2 * n_heads, CHUNK), lambda b, i: (b, i, 0, 0)),
        ],
        out_specs=pl.BlockSpec((1, ts, d), lambda b, i: (b, i, 0)),
        out_shape=jax.ShapeDtypeStruct((bsz, seq, d), BF16),
        scratch_shapes=[pltpu.VMEM((n_heads, head_dim, head_dim), F32)],
        compiler_params=_cparams(("parallel", "arbitrary")),
        name="gdn_delta_rule",
    )(qkv3, qkv3, qkv3, cols, rows)


def _route_rows(logits_t, bias):
    n_exp = logits_t.shape[0]
    mx = jnp.max(logits_t, axis=0, keepdims=True)
    ex = jnp.exp(logits_t - mx)
    probs = ex / jnp.sum(ex, axis=0, keepdims=True)
    sel = probs + bias
    sel_r = [sel[e:e + 1, :] for e in range(n_exp)]
    prob_r = [probs[e:e + 1, :] for e in range(n_exp)]

    def top2_sum(a, b, c, d):
        hi1, lo1 = jnp.maximum(a, b), jnp.minimum(a, b)
        hi2, lo2 = jnp.maximum(c, d), jnp.minimum(c, d)
        return jnp.maximum(hi1, hi2) + jnp.maximum(jnp.minimum(hi1, hi2), jnp.maximum(lo1, lo2))

    scores = [top2_sum(*sel_r[EXPERTS_PER_GROUP * g:EXPERTS_PER_GROUP * (g + 1)])
              for g in range(N_GROUPS)]
    best = scores[0]
    gidx = jnp.zeros(best.shape, jnp.int32)
    for g in range(1, N_GROUPS):
        upd = scores[g] > best
        best = jnp.where(upd, scores[g], best)
        gidx = jnp.where(upd, g, gidx)

    def pick(vals, j):
        out = vals[j]
        for g in range(1, N_GROUPS):
            out = jnp.where(gidx == g, vals[EXPERTS_PER_GROUP * g + j], out)
        return out

    cand = [pick(sel_r, j) for j in range(EXPERTS_PER_GROUP)]
    cprob = [pick(prob_r, j) for j in range(EXPERTS_PER_GROUP)]
    b1, i1, p1 = cand[0], jnp.zeros(best.shape, jnp.int32), cprob[0]
    for j in range(1, EXPERTS_PER_GROUP):
        upd = cand[j] > b1
        b1 = jnp.where(upd, cand[j], b1)
        i1 = jnp.where(upd, j, i1)
        p1 = jnp.where(upd, cprob[j], p1)
    b2 = jnp.full(best.shape, -jnp.inf, F32)
    i2 = jnp.full(best.shape, -1, jnp.int32)
    p2 = jnp.zeros(best.shape, F32)
    for j in range(EXPERTS_PER_GROUP):
        upd = jnp.logical_and(i1 != j, jnp.logical_or(cand[j] > b2, i2 < 0))
        b2 = jnp.where(upd, cand[j], b2)
        i2 = jnp.where(upd, j, i2)
        p2 = jnp.where(upd, cprob[j], p2)
    tot = p1 + p2
    base = gidx * EXPERTS_PER_GROUP
    return base + i1, base + i2, p1 / tot, p2 / tot


def _mix_out_kernel(o_ref, og_ref, h_ref, w_ref, lng_ref, lnb_ref, rw_ref, rb_ref,
                    *rest, alpha, gdn_head_dim):
    if gdn_head_dim:
        ng_ref, hn_ref, route_ref = rest
    else:
        hn_ref, route_ref = rest
    o = o_ref[...].astype(F32)
    og = og_ref[...].astype(F32)
    if gdn_head_dim:
        parts = []
        for j in range(o.shape[1] // gdn_head_dim):
            x = o[:, j * gdn_head_dim:(j + 1) * gdn_head_dim]
            r = lax.rsqrt(jnp.mean(x * x, axis=-1, keepdims=True) + RMS_EPS)
            parts.append(x * r * ng_ref[...])
        a = jnp.concatenate(parts, axis=1) * _silu(og)
    else:
        a = o * _sigmoid(og)
    mix = jnp.dot(a.astype(BF16), w_ref[...], preferred_element_type=F32)
    hn = _layer_norm_rows(alpha * h_ref[...] + mix, lng_ref[...], lnb_ref[...])
    hn_ref[...] = hn
    logits_t = lax.dot_general(rw_ref[...], hn, (((1,), (1,)), ((), ())),
                               preferred_element_type=F32,
                               precision=lax.Precision.HIGHEST)
    e1, e2, g1, g2 = _route_rows(logits_t, rb_ref[...])
    route_ref[0:1, :] = e1.astype(F32)
    route_ref[1:2, :] = e2.astype(F32)
    route_ref[2:3, :] = g1
    route_ref[3:4, :] = g2
    route_ref[4:8, :] = jnp.zeros((4, g1.shape[1]), F32)


def _mix_out(o, proj, og_col_block, h, w_out, ln_g, ln_b, rw_t, rb, alpha,
             norm_g=None, gdn_head_dim=0):
    n, d = h.shape
    n_exp = rw_t.shape[0]
    tm = min(ROW_TILE, n)
    row = lambda i: (i, 0)
    const = lambda i: (0, 0)
    in_specs = [
        pl.BlockSpec((tm, d), row),
        pl.BlockSpec((tm, d), lambda i: (i, og_col_block)),
        pl.BlockSpec((tm, d), row),
        pl.BlockSpec((d, d), const),
        pl.BlockSpec((1, d), const),
        pl.BlockSpec((1, d), const),
        pl.BlockSpec((n_exp, d), const),
        pl.BlockSpec((n_exp, 1), const),
    ]
    args = [o, proj, h, w_out, ln_g.reshape(1, d), ln_b.reshape(1, d), rw_t,
            rb.reshape(n_exp, 1)]
    if gdn_head_dim:
        in_specs.append(pl.BlockSpec((1, gdn_head_dim), const))
        args.append(norm_g.reshape(1, gdn_head_dim))
    return pl.pallas_call(
        functools.partial(_mix_out_kernel, alpha=alpha, gdn_head_dim=gdn_head_dim),
        grid=(n // tm,),
        in_specs=in_specs,
        out_specs=[pl.BlockSpec((tm, d), row), pl.BlockSpec((8, tm), lambda i: (0, i))],
        out_shape=[jax.ShapeDtypeStruct((n, d), F32), jax.ShapeDtypeStruct((8, n), F32)],
        compiler_params=_cparams(("parallel",)),
        name="mix_out",
    )(*args)


def _moe_kernel(elo_ref, ehi_ref, nblk_ref, x_ref, gate_ref, wgu_lo_ref, wd_lo_ref,
                wgu_hi_ref, wd_hi_ref, lng_ref, lnb_ref, o_ref, *, alpha, d_expert):
    del elo_ref, ehi_ref

    @pl.when(pl.program_id(0) < nblk_ref[0])
    def _():
        x = x_ref[...]
        xb = x.astype(BF16)

        def ffn(wgu_ref, wd_ref):
            a = jnp.dot(xb, wgu_ref[0], preferred_element_type=F32)
            hid = _silu(a[:, :d_expert]) * a[:, d_expert:]
            return jnp.dot(hid.astype(BF16), wd_ref[0], preferred_element_type=F32)

        y = (ffn(wgu_lo_ref, wd_lo_ref) * gate_ref[:, 0:1]
             + ffn(wgu_hi_ref, wd_hi_ref) * gate_ref[:, 1:2])
        o_ref[...] = _layer_norm_rows(alpha * x + y, lng_ref[...], lnb_ref[...])


def _moe_ffn(x_sorted, gates_sorted, blk_lo, blk_hi, n_used, wgu, wd, ln_g, ln_b, alpha):
    n_slots, d = x_sorted.shape
    d_expert = wd.shape[1]
    n_blocks = n_slots // MOE_BLOCK
    grid_spec = pltpu.PrefetchScalarGridSpec(
        num_scalar_prefetch=3,
        grid=(n_blocks,),
        in_specs=[
            pl.BlockSpec((MOE_BLOCK, d), lambda i, lo, hi, nb: (i, 0)),
            pl.BlockSpec((MOE_BLOCK, 2), lambda i, lo, hi, nb: (i, 0)),
            pl.BlockSpec((1, d, 2 * d_expert), lambda i, lo, hi, nb: (lo[i], 0, 0)),
            pl.BlockSpec((1, d_expert, d), lambda i, lo, hi, nb: (lo[i], 0, 0)),
            pl.BlockSpec((1, d, 2 * d_expert), lambda i, lo, hi, nb: (hi[i], 0, 0)),
            pl.BlockSpec((1, d_expert, d), lambda i, lo, hi, nb: (hi[i], 0, 0)),
            pl.BlockSpec((1, d), lambda i, lo, hi, nb: (0, 0)),
            pl.BlockSpec((1, d), lambda i, lo, hi, nb: (0, 0)),
        ],
        out_specs=pl.BlockSpec((MOE_BLOCK, d), lambda i, lo, hi, nb: (i, 0)),
    )
    return pl.pallas_call(
        functools.partial(_moe_kernel, alpha=alpha, d_expert=d_expert),
        grid_spec=grid_spec,
        out_shape=jax.ShapeDtypeStruct((n_slots, d), F32),
        compiler_params=_cparams(("arbitrary",)),
        name="moe_ffn",
    )(blk_lo, blk_hi, n_used, x_sorted, gates_sorted, wgu, wd, wgu, wd,
      ln_g.reshape(1, d), ln_b.reshape(1, d))


def _dispatch_plan(route, n_tok):
    e1 = route[0].astype(jnp.int32)
    e2 = route[1].astype(jnp.int32)
    first_is_lo = e1 < e2
    lo = jnp.where(first_is_lo, e1, e2)
    hi = jnp.where(first_is_lo, e2, e1)
    g_lo = jnp.where(first_is_lo, route[2], route[3])
    g_hi = jnp.where(first_is_lo, route[3], route[2])
    grp = lo // EXPERTS_PER_GROUP
    a = lo % EXPERTS_PER_GROUP
    b = hi % EXPERTS_PER_GROUP
    cls = grp * PAIRS_PER_GROUP + (a * (7 - a)) // 2 + (b - a - 1)
    n_cls = N_GROUPS * PAIRS_PER_GROUP
    n_blocks = -(-(n_tok + n_cls * (MOE_BLOCK - 1)) // MOE_BLOCK)
    n_slots = n_blocks * MOE_BLOCK

    counts = jnp.bincount(cls, length=n_cls)
    start = jnp.cumsum(counts) - counts
    padded = (counts + MOE_BLOCK - 1) // MOE_BLOCK * MOE_BLOCK
    pend = jnp.cumsum(padded)
    pstart = pend - padded
    order = jnp.argsort(cls, stable=True).astype(jnp.int32)
    cls_s = cls[order]
    slot = (pstart[cls_s] + (jnp.arange(n_tok) - start[cls_s])).astype(jnp.int32)
    slot_tok = jnp.zeros((n_slots,), jnp.int32).at[slot].set(order)
    tok_slot = jnp.zeros((n_tok,), jnp.int32).at[order].set(slot)
    blk_cls = jnp.minimum(jnp.searchsorted(pend, jnp.arange(n_blocks) * MOE_BLOCK, side='right'),
                          n_cls - 1)
    pair_lo = jnp.array([0, 0, 0, 1, 1, 2], jnp.int32)
    pair_hi = jnp.array([1, 2, 3, 2, 3, 3], jnp.int32)
    blk_grp = blk_cls // PAIRS_PER_GROUP
    blk_pair = blk_cls % PAIRS_PER_GROUP
    blk_lo = (blk_grp * EXPERTS_PER_GROUP + pair_lo[blk_pair]).astype(jnp.int32)
    blk_hi = (blk_grp * EXPERTS_PER_GROUP + pair_hi[blk_pair]).astype(jnp.int32)
    n_used = (pend[-1] // MOE_BLOCK).astype(jnp.int32).reshape(1)
    gates = jnp.stack([g_lo, g_hi], axis=1)
    return slot_tok, tok_slot, blk_lo, blk_hi, n_used, gates


def _moe_layer(h, route, wgu, wd, ln_g, ln_b, alpha):
    n_tok = h.shape[0]
    slot_tok, tok_slot, blk_lo, blk_hi, n_used, gates = _dispatch_plan(route, n_tok)
    x_sorted = jnp.take(h, slot_tok, axis=0)
    gates_sorted = jnp.take(gates, slot_tok, axis=0)
    out_sorted = _moe_ffn(x_sorted, gates_sorted, blk_lo, blk_hi, n_used, wgu, wd,
                          ln_g, ln_b, alpha)
    return jnp.take(out_sorted, tok_slot, axis=0)


def _pad_cols(w, width):
    return jnp.pad(w, ((0, 0), (0, width - w.shape[1])))


def kernel(x, ln_mix_g, ln_mix_b, ln_ffn_g, ln_ffn_b, router_w, router_b, fox_w_in, fox_b_f,
           fox_q_gain, fox_k_gain, fox_w_out, gdn_w_in, gdn_conv_w, gdn_a_log, gdn_dt_bias,
           gdn_norm_g, gdn_w_out, moe_w_gate, moe_w_up, moe_w_down):
    bsz, seq, d = x.shape
    n = bsz * seq
    depth = ln_mix_g.shape[0]
    alpha = (2.0 * depth) ** 0.25
    fox_heads = fox_b_f.shape[1]
    fox_dim = d // fox_heads
    gdn_heads = gdn_a_log.shape[1]
    gdn_dim = d // gdn_heads
    rw_t = router_w.T

    h = x.reshape(n, d)
    for i in range(depth):
        j = i // 2
        if i % 2 == 0:
            w_in = fox_w_in[j]
            proj, side = _inproj(h, w_in[:, :4 * d].astype(BF16),
                                 _pad_cols(w_in[:, 4 * d:], LANES).astype(BF16))
            gains = jnp.concatenate([jnp.tile(fox_q_gain[j], fox_heads) * fox_dim ** -0.5,
                                     jnp.tile(fox_k_gain[j], fox_heads)]).reshape(1, 2 * d)
            qk = _fox_prep(proj, gains, fox_dim)
            f_t = side[:, :fox_heads].reshape(bsz, seq, fox_heads).transpose(0, 2, 1)
            c = _fox_gates(f_t, fox_b_f[j])
            c_row = c.reshape(bsz, fox_heads // 2, 2, seq)
            c_col = c_row.transpose(0, 1, 3, 2)
            o = _fox_attention(qk, proj, c_col, c_row, bsz, seq, fox_heads, fox_dim)
            h, route = _mix_out(o.reshape(n, d), proj, 3, h, fox_w_out[j].astype(BF16),
                                ln_mix_g[i], ln_mix_b[i], rw_t, router_b, alpha)
        else:
            w_in = gdn_w_in[j]
            proj, side = _inproj(h, w_in[:, :4 * d].astype(BF16),
                                 _pad_cols(w_in[:, 4 * d:], LANES).astype(BF16))
            qkv = _gdn_prep(proj.reshape(bsz, seq, 4 * d), gdn_conv_w[j], d, gdn_dim)
            side3 = side.reshape(bsz, seq, LANES)
            b_t = side3[:, :, :gdn_heads].transpose(0, 2, 1)
            a_t = side3[:, :, gdn_heads:2 * gdn_heads].transpose(0, 2, 1)
            beta, gc = _gdn_gates(b_t, a_t, gdn_a_log[j], gdn_dt_bias[j])
            rows = jnp.concatenate([beta, gc], axis=1).reshape(
                bsz, 2 * gdn_heads, seq // CHUNK, CHUNK).transpose(0, 2, 1, 3)
            cols = rows.transpose(0, 1, 3, 2)
            o = _gdn_delta_rule(qkv, cols, rows, gdn_heads, gdn_dim)
            h, route = _mix_out(o.reshape(n, d), proj, 3, h, gdn_w_out[j].astype(BF16),
                                ln_mix_g[i], ln_mix_b[i], rw_t, router_b, alpha,
                                norm_g=gdn_norm_g[j], gdn_head_dim=gdn_dim)
        wgu = jnp.concatenate([moe_w_gate[i], moe_w_up[i]], axis=2).astype(BF16)
        h = _moe_layer(h, route, wgu, moe_w_down[i].astype(BF16), ln_ffn_g[i], ln_ffn_b[i], alpha)
    return h.reshape(bsz, seq, d)
```

```python
import functools
import math

import jax
import jax.numpy as jnp
from jax import lax
from jax.experimental import pallas as pl
from jax.experimental.pallas import tpu as pltpu

F32 = jnp.float32
BF16 = jnp.bfloat16

N_GROUPS = 4
EXPERTS_PER_GROUP = 4
PAIRS_PER_GROUP = 6
CHUNK = 64
CONV_WIDTH = 4
LN_EPS = 1e-5
RMS_EPS = 1e-6
L2_EPS = 1e-6

LANES = 128
VMEM_LIMIT = 56 * 1024 * 1024

ROW_TILE = 512
ATT_Q_TILE = 256
ATT_K_TILE = 512
GDN_TILE = 256
MOE_BLOCK = 256


def _cparams(sem):
    return pltpu.CompilerParams(dimension_semantics=sem, vmem_limit_bytes=VMEM_LIMIT)


def _layer_norm_rows(y, g, b):
    mu = jnp.mean(y, axis=-1, keepdims=True)
    yc = y - mu
    var = jnp.mean(yc * yc, axis=-1, keepdims=True)
    return yc * lax.rsqrt(var + LN_EPS) * g + b


def _sigmoid(x):
    return 1.0 / (1.0 + jnp.exp(-x))


def _silu(x):
    return x * _sigmoid(x)


def _softplus(x):
    return jnp.maximum(x, 0.0) + jnp.log(1.0 + jnp.exp(-jnp.abs(x)))


def _inproj_kernel(x_ref, w_ref, ws_ref, o_ref, os_ref, *, col_chunk):
    xb = x_ref[...].astype(BF16)
    width = o_ref.shape[1]
    for j in range(width // col_chunk):
        sl = slice(j * col_chunk, (j + 1) * col_chunk)
        o_ref[:, sl] = jnp.dot(xb, w_ref[:, sl], preferred_element_type=F32).astype(BF16)
    os_ref[...] = jnp.dot(xb, ws_ref[...], preferred_element_type=F32)


def _inproj(x, w_main, w_side):
    n, d = x.shape
    width = w_main.shape[1]
    tm = min(ROW_TILE, n)
    return pl.pallas_call(
        functools.partial(_inproj_kernel, col_chunk=1024),
        grid=(n // tm,),
        in_specs=[
            pl.BlockSpec((tm, d), lambda i: (i, 0)),
            pl.BlockSpec((d, width), lambda i: (0, 0)),
            pl.BlockSpec((d, LANES), lambda i: (0, 0)),
        ],
        out_specs=[
            pl.BlockSpec((tm, width), lambda i: (i, 0)),
            pl.BlockSpec((tm, LANES), lambda i: (i, 0)),
        ],
        out_shape=[
            jax.ShapeDtypeStruct((n, width), BF16),
            jax.ShapeDtypeStruct((n, LANES), F32),
        ],
        compiler_params=_cparams(("parallel",)),
        name="inproj",
    )(x, w_main, w_side)


def _fox_prep_kernel(p_ref, gain_ref, qk_ref, vt_ref, *, head_dim):
    lane = lax.broadcasted_iota(jnp.int32, (1, LANES), 1)
    first = lane < head_dim
    qk_width = qk_ref.shape[2]
    for j in range(qk_width // LANES):
        sl = slice(j * LANES, (j + 1) * LANES)
        x = p_ref[0, :, sl].astype(F32)
        xx = x * x
        s_a = jnp.sum(jnp.where(first, xx, 0.0), axis=-1, keepdims=True)
        s_b = jnp.sum(jnp.where(first, 0.0, xx), axis=-1, keepdims=True)
        r = jnp.where(first, lax.rsqrt(s_a / head_dim + RMS_EPS),
                      lax.rsqrt(s_b / head_dim + RMS_EPS))
        qk_ref[0, :, sl] = (x * r * gain_ref[:, sl]).astype(BF16)
    for j in range(vt_ref.shape[1] // LANES):
        v = p_ref[0, :, qk_width + j * LANES:qk_width + (j + 1) * LANES].astype(F32)
        vt_ref[0, j * LANES:(j + 1) * LANES, :] = v.T.astype(BF16)


def _fox_prep(proj3, gains, head_dim):
    bsz, seq, _ = proj3.shape
    qk_width = gains.shape[1]
    d = qk_width // 2
    ts = min(ROW_TILE, seq)
    return pl.pallas_call(
        functools.partial(_fox_prep_kernel, head_dim=head_dim),
        grid=(bsz, seq // ts),
        in_specs=[
            pl.BlockSpec((1, ts, qk_width + d), lambda b, i: (b, i, 0)),
            pl.BlockSpec((1, qk_width), lambda b, i: (0, 0)),
        ],
        out_specs=[
            pl.BlockSpec((1, ts, qk_width), lambda b, i: (b, i, 0)),
            pl.BlockSpec((1, d, ts), lambda b, i: (b, 0, i)),
        ],
        out_shape=[
            jax.ShapeDtypeStruct((bsz, seq, qk_width), BF16),
            jax.ShapeDtypeStruct((bsz, d, seq), BF16),
        ],
        compiler_params=_cparams(("parallel", "parallel")),
        name="fox_prep",
    )(proj3, gains)


def _prefix_sum_lanes(x, period):
    pos = lax.broadcasted_iota(jnp.int32, x.shape, 1) % period
    d = 1
    while d < period:
        x = x + jnp.where(pos >= d, pltpu.roll(x, d, axis=1), 0.0)
        d *= 2
    return x


def _fox_gate_kernel(f_ref, b_ref, c_ref):
    z = f_ref[0] + b_ref[...]
    log_f = -_softplus(-z)
    c_ref[0] = _prefix_sum_lanes(log_f, z.shape[1])


def _fox_gates(f_logit_t, b_f):
    bsz, h, s = f_logit_t.shape
    return pl.pallas_call(
        _fox_gate_kernel,
        grid=(bsz,),
        in_specs=[pl.BlockSpec((1, h, s), lambda b: (b, 0, 0)),
                  pl.BlockSpec((h, 1), lambda b: (0, 0))],
        out_specs=pl.BlockSpec((1, h, s), lambda b: (b, 0, 0)),
        out_shape=jax.ShapeDtypeStruct((bsz, h, s), F32),
        compiler_params=_cparams(("parallel",)),
        name="fox_gates",
    )(f_logit_t, b_f.reshape(h, 1))


def _gdn_gate_kernel(b_ref, a_ref, alog_ref, dt_ref, beta_ref, gc_ref):
    beta_ref[0] = _sigmoid(b_ref[0])
    g = -jnp.exp(alog_ref[...]) * _softplus(a_ref[0] + dt_ref[...])
    gc_ref[0] = _prefix_sum_lanes(g, CHUNK)


def _gdn_gates(b_logit_t, a_logit_t, a_log, dt_bias):
    bsz, h, s = b_logit_t.shape
    spec = pl.BlockSpec((1, h, s), lambda b: (b, 0, 0))
    vec = pl.BlockSpec((h, 1), lambda b: (0, 0))
    return pl.pallas_call(
        _gdn_gate_kernel,
        grid=(bsz,),
        in_specs=[spec, spec, vec, vec],
        out_specs=[spec, spec],
        out_shape=[jax.ShapeDtypeStruct((bsz, h, s), F32)] * 2,
        compiler_params=_cparams(("parallel",)),
        name="gdn_gates",
    )(b_logit_t, a_logit_t, a_log.reshape(h, 1), dt_bias.reshape(h, 1))


def _fox_attn_kernel(q_ref, k_ref, vt_ref, crow_ref, o_ref, ckb_ref, acc_ref, sbuf_ref,
                     *, head_dim, tq, tk):
    qi = pl.program_id(2)
    seq = k_ref.shape[1]

    @pl.when(qi == 0)
    def _():
        def fill(j, carry):
            st = pl.multiple_of(j * LANES, LANES)
            rows = crow_ref[0, 0, :, pl.ds(st, LANES)]
            for hh in range(2):
                ckb_ref[hh, pl.ds(st, LANES), :] = jnp.broadcast_to(
                    rows[hh:hh + 1, :], (LANES, LANES)).T
            return carry
        lax.fori_loop(0, seq // LANES, fill, 0)

    lane = lax.broadcasted_iota(jnp.int32, (1, LANES), 1)
    first = lane < head_dim
    q = q_ref[0]
    zero = jnp.zeros_like(q)
    q_heads = (jnp.where(first, q, zero), jnp.where(first, zero, q))
    q_start = pl.multiple_of(qi * tq, tq)
    c_q = crow_ref[0, 0, :, pl.ds(q_start, tq)]
    acc_ref[...] = jnp.zeros(acc_ref.shape, F32)

    def scores(j):
        start = pl.multiple_of(j * tk, tk)
        k = k_ref[0, pl.ds(start, tk), :]
        return [lax.dot_general(k, q_heads[hh], (((1,), (1,)), ((), ())),
                                preferred_element_type=F32) for hh in range(2)]

    def update(j, s_in, stats, masked):
        start = pl.multiple_of(j * tk, tk)
        out = []
        for hh in range(2):
            m_old, l_old = stats[2 * hh], stats[2 * hh + 1]
            s0 = s_in[hh]
            cb = ckb_ref[hh, pl.ds(start, tk), :]
            slabs = []
            for t in range(tq // LANES):
                sl = s0[:, t * LANES:(t + 1) * LANES] - cb
                if masked:
                    kpos = start + lax.broadcasted_iota(jnp.int32, sl.shape, 0)
                    qpos = q_start + t * LANES + lax.broadcasted_iota(jnp.int32, sl.shape, 1)
                    sl = jnp.where(kpos <= qpos, sl, -jnp.inf)
                slabs.append(sl)
            s = jnp.concatenate(slabs, axis=1)
            cq = c_q[hh:hh + 1, :]
            m_new = jnp.maximum(m_old, jnp.max(s, axis=0, keepdims=True) + cq)
            p = jnp.exp(s + (cq - m_new))
            alpha = jnp.exp(m_old - m_new)
            l_new = alpha * l_old + jnp.sum(p, axis=0, keepdims=True)
            vt = vt_ref[0, hh * head_dim:(hh + 1) * head_dim, pl.ds(start, tk)]
            pv = jnp.dot(vt, p.astype(BF16), preferred_element_type=F32)
            acc_ref[hh] = acc_ref[hh] * alpha + pv
            out += [m_new, l_new]
        return tuple(out)

    def park(s):
        sbuf_ref[0] = s[0]
        sbuf_ref[1] = s[1]

    def parked():
        return [sbuf_ref[0], sbuf_ref[1]]

    def pair(i, stats):
        j0 = 2 * i
        s1 = scores(j0 + 1)
        stats = update(j0, parked(), stats, False)
        park(scores(j0 + 2))
        return update(j0 + 1, s1, stats, False)

    neg = jnp.full((1, tq), -1e30, F32)
    zer = jnp.zeros((1, tq), F32)
    n_full = (qi * tq) // tk
    n_pairs = n_full // 2
    park(scores(0))
    stats = lax.fori_loop(0, n_pairs, pair, (neg, zer, neg, zer))
    j0 = 2 * n_pairs

    def tail_two(stats):
        s1 = scores(j0 + 1)
        stats = update(j0, parked(), stats, False)
        return update(j0 + 1, s1, stats, True)

    def tail_one(stats):
        return update(j0, parked(), stats, True)

    stats = lax.cond(n_full - j0 == 1, tail_two, tail_one, stats)
    o_t = jnp.concatenate([acc_ref[0] / stats[1], acc_ref[1] / stats[3]], axis=0)
    o_ref[0] = o_t.T.astype(BF16)


def _fox_attention(qk3, v_t, c_row, n_heads, head_dim):
    bsz, seq, _ = qk3.shape
    d = n_heads * head_dim
    pairs = d // LANES
    tq = min(ATT_Q_TILE, seq)
    tk = min(ATT_K_TILE, seq)
    assert tk % tq == 0 and seq % tk == 0
    return pl.pallas_call(
        functools.partial(_fox_attn_kernel, head_dim=head_dim, tq=tq, tk=tk),
        grid=(bsz, pairs, seq // tq),
        in_specs=[
            pl.BlockSpec((1, tq, LANES), lambda b, p, i: (b, i, p)),
            pl.BlockSpec((1, seq, LANES), lambda b, p, i: (b, 0, pairs + p)),
            pl.BlockSpec((1, LANES, seq), lambda b, p, i: (b, p, 0)),
            pl.BlockSpec((1, 1, 2, seq), lambda b, p, i: (b, p, 0, 0)),
        ],
        out_specs=pl.BlockSpec((1, tq, LANES), lambda b, p, i: (b, i, p)),
        out_shape=jax.ShapeDtypeStruct((bsz, seq, d), BF16),
        scratch_shapes=[
            pltpu.VMEM((2, seq, LANES), F32),
            pltpu.VMEM((2, head_dim, tq), F32),
            pltpu.VMEM((2, tk, tq), F32),
        ],
        compiler_params=_cparams(("parallel", "parallel", "arbitrary")),
        name="fox_attention",
    )(qk3, qk3, v_t, c_row)


def _gdn_prep_kernel(p_ref, w_ref, o_ref, buf_ref, *, d_model, head_dim, q_scale):
    ts = p_ref.shape[1]
    pad = 8

    @pl.when(pl.program_id(1) == 0)
    def _():
        buf_ref[0:pad, :] = jnp.zeros((pad, buf_ref.shape[1]), F32)

    buf_ref[pad:pad + ts, :] = p_ref[0].astype(F32)
    for j in range(o_ref.shape[2] // LANES):
        sl = slice(j * LANES, (j + 1) * LANES)
        acc = buf_ref[pad:pad + ts, sl] * w_ref[CONV_WIDTH - 1:CONV_WIDTH, sl]
        for dlt in range(1, CONV_WIDTH):
            acc = acc + (buf_ref[pad - dlt:pad - dlt + ts, sl]
                         * w_ref[CONV_WIDTH - 1 - dlt:CONV_WIDTH - dlt, sl])
        y = _silu(acc)
        if j * LANES < 2 * d_model:
            y = y * lax.rsqrt(jnp.sum(y * y, axis=-1, keepdims=True) + L2_EPS)
            if j * LANES < d_model:
                y = y * q_scale
        o_ref[0, :, sl] = y.astype(BF16)
    buf_ref[0:pad, :] = buf_ref[ts:ts + pad, :]


def _gdn_prep(proj3, conv_w, d_model, head_dim):
    bsz, seq, _ = proj3.shape
    width = 3 * d_model
    ts = min(ROW_TILE, seq)
    return pl.pallas_call(
        functools.partial(_gdn_prep_kernel, d_model=d_model, head_dim=head_dim,
                          q_scale=head_dim ** -0.5),
        grid=(bsz, seq // ts),
        in_specs=[
            pl.BlockSpec((1, ts, width), lambda b, i: (b, i, 0)),
            pl.BlockSpec((CONV_WIDTH, width), lambda b, i: (0, 0)),
        ],
        out_specs=pl.BlockSpec((1, ts, width), lambda b, i: (b, i, 0)),
        out_shape=jax.ShapeDtypeStruct((bsz, seq, width), BF16),
        scratch_shapes=[pltpu.VMEM((ts + 8, width), F32)],
        compiler_params=_cparams(("parallel", "arbitrary")),
        name="gdn_prep",
    )(proj3, conv_w)


def _gdn_chunk_kernel(q_ref, k_ref, v_ref, col_ref, row_ref, o_ref, state_ref,
                      *, n_heads, head_dim):
    @pl.when(pl.program_id(1) == 0)
    def _():
        state_ref[...] = jnp.zeros(state_ref.shape, F32)

    n_chunks = q_ref.shape[1] // CHUNK
    ri = lax.broadcasted_iota(jnp.int32, (CHUNK, CHUNK), 0)
    ci = lax.broadcasted_iota(jnp.int32, (CHUNK, CHUNK), 1)
    eye = jnp.where(ri == ci, 1.0, 0.0)
    heads = range(n_heads)
    nt = (((1,), (1,)), ((), ()))

    def mm(a, b):
        return jnp.dot(a, b, preferred_element_type=F32)

    def stack(a, b):
        return jnp.concatenate([a, b], axis=0)

    def chunk(c, carry):
        start = pl.multiple_of(c * CHUNK, CHUNK)
        cols = col_ref[0, c]
        rows = row_ref[0, c]
        sls = [slice(h * head_dim, (h + 1) * head_dim) for h in heads]
        q = [q_ref[0, pl.ds(start, CHUNK), sls[h]] for h in heads]
        k = [k_ref[0, pl.ds(start, CHUNK), sls[h]] for h in heads]
        v = [v_ref[0, pl.ds(start, CHUNK), sls[h]] for h in heads]
        beta = [cols[:, h:h + 1] for h in heads]
        g_col = [cols[:, n_heads + h:n_heads + h + 1] for h in heads]
        g_row = [rows[n_heads + h:n_heads + h + 1, :] for h in heads]
        g_last = [g_col[h][CHUNK - 1:CHUNK, :] for h in heads]
        decay = [jnp.exp(jnp.where(ri >= ci, g_col[h] - g_row[h], -jnp.inf)) for h in heads]
        kf = [k[h].astype(F32) for h in heads]
        kb = [kf[h] * beta[h] for h in heads]

        r1 = [lax.dot_general(stack(kb[h].astype(BF16), q[h]), k[h], nt,
                              preferred_element_type=F32) for h in heads]
        m = [jnp.where(ri > ci, r1[h][:CHUNK] * decay[h], 0.0) for h in heads]
        attn = [(r1[h][CHUNK:] * decay[h]).astype(BF16) for h in heads]

        t = [eye - m[h] for h in heads]
        pb = [m[h].astype(BF16) for h in heads]
        p = [mm(pb[h], pb[h]) for h in heads]
        power = 2
        while power < CHUNK:
            pb = [p[h].astype(BF16) for h in heads]
            if 2 * power >= CHUNK:
                t = [t[h] + mm(t[h].astype(BF16), pb[h]) for h in heads]
            else:
                r = [mm(stack(t[h].astype(BF16), pb[h]), pb[h]) for h in heads]
                t = [t[h] + r[h][:CHUNK] for h in heads]
                p = [r[h][CHUNK:] for h in heads]
            power *= 2

        e_g = [jnp.exp(g_col[h]) for h in heads]
        rhs = [jnp.concatenate([v[h].astype(F32) * beta[h], kb[h] * e_g[h]], axis=1).astype(BF16)
               for h in heads]
        sol = [mm(t[h].astype(BF16), rhs[h]) for h in heads]
        q_dec = [(q[h].astype(F32) * e_g[h]).astype(BF16) for h in heads]
        k_dec_t = [(kf[h] * jnp.exp(g_last[h] - g_col[h])).T.astype(BF16) for h in heads]

        sb = [state_ref[h].astype(BF16) for h in heads]
        r2 = [mm(stack(sol[h][:, head_dim:].astype(BF16), q_dec[h]), sb[h]) for h in heads]
        vb = [(sol[h][:, :head_dim] - r2[h][:CHUNK]).astype(BF16) for h in heads]
        r3 = [mm(stack(attn[h], k_dec_t[h]), vb[h]) for h in heads]
        for h in heads:
            o_ref[0, pl.ds(start, CHUNK), sls[h]] = (r2[h][CHUNK:] + r3[h][:CHUNK]).astype(BF16)
            state_ref[h] = state_ref[h] * jnp.exp(g_last[h]) + r3[h][CHUNK:]
        return carry

    lax.fori_loop(0, n_chunks, chunk, 0)


def _gdn_delta_rule(qkv3, cols, rows, n_heads, head_dim):
    bsz, seq, _ = qkv3.shape
    d = n_heads * head_dim
    ts = min(GDN_TILE, seq)
    cpt = ts // CHUNK
    return pl.pallas_call(
        functools.partial(_gdn_chunk_kernel, n_heads=n_heads, head_dim=head_dim),
        grid=(bsz, seq // ts),
        in_specs=[
            pl.BlockSpec((1, ts, d), lambda b, i: (b, i, 0)),
            pl.BlockSpec((1, ts, d), lambda b, i: (b, i, 1)),
            pl.BlockSpec((1, ts, d), lambda b, i: (b, i, 2)),
            pl.BlockSpec((1, cpt, CHUNK, 2 * n_heads), lambda b, i: (b, i, 0, 0)),
            pl.BlockSpec((1, cpt, 2 * n_heads, CHUNK), lambda b, i: (b, i, 0, 0)),
        ],
        out_specs=pl.BlockSpec((1, ts, d), lambda b, i: (b, i, 0)),
        out_shape=jax.ShapeDtypeStruct((bsz, seq, d), BF16),
        scratch_shapes=[pltpu.VMEM((n_heads, head_dim, head_dim), F32)],
        compiler_params=_cparams(("parallel", "arbitrary")),
        name="gdn_delta_rule",
    )(qkv3, qkv3, qkv3, cols, rows)


def _route_rows(logits_t, bias):
    n_exp = logits_t.shape[0]
    mx = jnp.max(logits_t, axis=0, keepdims=True)
    ex = jnp.exp(logits_t - mx)
    probs = ex / jnp.sum(ex, axis=0, keepdims=True)
    sel = probs + bias
    sel_r = [sel[e:e + 1, :] for e in range(n_exp)]
    prob_r = [probs[e:e + 1, :] for e in range(n_exp)]

    def top2_sum(a, b, c, d):
        hi1, lo1 = jnp.maximum(a, b), jnp.minimum(a, b)
        hi2, lo2 = jnp.maximum(c, d), jnp.minimum(c, d)
        return jnp.maximum(hi1, hi2) + jnp.maximum(jnp.minimum(hi1, hi2), jnp.maximum(lo1, lo2))

    scores = [top2_sum(*sel_r[EXPERTS_PER_GROUP * g:EXPERTS_PER_GROUP * (g + 1)])
              for g in range(N_GROUPS)]
    best = scores[0]
    gidx = jnp.zeros(best.shape, jnp.int32)
    for g in range(1, N_GROUPS):
        upd = scores[g] > best
        best = jnp.where(upd, scores[g], best)
        gidx = jnp.where(upd, g, gidx)

    def pick(vals, j):
        out = vals[j]
        for g in range(1, N_GROUPS):
            out = jnp.where(gidx == g, vals[EXPERTS_PER_GROUP * g + j], out)
        return out

    cand = [pick(sel_r, j) for j in range(EXPERTS_PER_GROUP)]
    cprob = [pick(prob_r, j) for j in range(EXPERTS_PER_GROUP)]
    b1, i1, p1 = cand[0], jnp.zeros(best.shape, jnp.int32), cprob[0]
    for j in range(1, EXPERTS_PER_GROUP):
        upd = cand[j] > b1
        b1 = jnp.where(upd, cand[j], b1)
        i1 = jnp.where(upd, j, i1)
        p1 = jnp.where(upd, cprob[j], p1)
    b2 = jnp.full(best.shape, -jnp.inf, F32)
    i2 = jnp.full(best.shape, -1, jnp.int32)
    p2 = jnp.zeros(best.shape, F32)
    for j in range(EXPERTS_PER_GROUP):
        upd = jnp.logical_and(i1 != j, jnp.logical_or(cand[j] > b2, i2 < 0))
        b2 = jnp.where(upd, cand[j], b2)
        i2 = jnp.where(upd, j, i2)
        p2 = jnp.where(upd, cprob[j], p2)
    tot = p1 + p2
    base = gidx * EXPERTS_PER_GROUP
    return base + i1, base + i2, p1 / tot, p2 / tot


def _mix_out_kernel(o_ref, og_ref, h_ref, w_ref, lng_ref, lnb_ref, rw_ref, rb_ref,
                    *rest, alpha, gdn_head_dim):
    if gdn_head_dim:
        ng_ref, hn_ref, route_ref = rest
    else:
        hn_ref, route_ref = rest
    o = o_ref[...].astype(F32)
    og = og_ref[...].astype(F32)
    if gdn_head_dim:
        parts = []
        for j in range(o.shape[1] // gdn_head_dim):
            x = o[:, j * gdn_head_dim:(j + 1) * gdn_head_dim]
            r = lax.rsqrt(jnp.mean(x * x, axis=-1, keepdims=True) + RMS_EPS)
            parts.append(x * r * ng_ref[...])
        a = jnp.concatenate(parts, axis=1) * _silu(og)
    else:
        a = o * _sigmoid(og)
    mix = jnp.dot(a.astype(BF16), w_ref[...], preferred_element_type=F32)
    hn = _layer_norm_rows(alpha * h_ref[...] + mix, lng_ref[...], lnb_ref[...])
    hn_ref[...] = hn
    logits_t = lax.dot_general(rw_ref[...], hn, (((1,), (1,)), ((), ())),
                               preferred_element_type=F32,
                               precision=lax.Precision.HIGHEST)
    e1, e2, g1, g2 = _route_rows(logits_t, rb_ref[...])
    route_ref[0:1, :] = e1.astype(F32)
    route_ref[1:2, :] = e2.astype(F32)
    route_ref[2:3, :] = g1
    route_ref[3:4, :] = g2
    route_ref[4:8, :] = jnp.zeros((4, g1.shape[1]), F32)


def _mix_out(o, proj, og_col_block, h, w_out, ln_g, ln_b, rw_t, rb, alpha,
             norm_g=None, gdn_head_dim=0):
    n, d = h.shape
    n_exp = rw_t.shape[0]
    tm = min(ROW_TILE, n)
    row = lambda i: (i, 0)
    const = lambda i: (0, 0)
    in_specs = [
        pl.BlockSpec((tm, d), row),
        pl.BlockSpec((tm, d), lambda i: (i, og_col_block)),
        pl.BlockSpec((tm, d), row),
        pl.BlockSpec((d, d), const),
        pl.BlockSpec((1, d), const),
        pl.BlockSpec((1, d), const),
        pl.BlockSpec((n_exp, d), const),
        pl.BlockSpec((n_exp, 1), const),
    ]
    args = [o, proj, h, w_out, ln_g.reshape(1, d), ln_b.reshape(1, d), rw_t,
            rb.reshape(n_exp, 1)]
    if gdn_head_dim:
        in_specs.append(pl.BlockSpec((1, gdn_head_dim), const))
        args.append(norm_g.reshape(1, gdn_head_dim))
    return pl.pallas_call(
        functools.partial(_mix_out_kernel, alpha=alpha, gdn_head_dim=gdn_head_dim),
        grid=(n // tm,),
        in_specs=in_specs,
        out_specs=[pl.BlockSpec((tm, d), row), pl.BlockSpec((8, tm), lambda i: (0, i))],
        out_shape=[jax.ShapeDtypeStruct((n, d), F32), jax.ShapeDtypeStruct((8, n), F32)],
        compiler_params=_cparams(("parallel",)),
        name="mix_out",
    )(*args)


def _moe_kernel(elo_ref, ehi_ref, nblk_ref, x_ref, gate_ref, wgu_lo_ref, wd_lo_ref,
                wgu_hi_ref, wd_hi_ref, lng_ref, lnb_ref, o_ref, *, alpha, d_expert):
    del elo_ref, ehi_ref

    @pl.when(pl.program_id(0) < nblk_ref[0])
    def _():
        x = x_ref[...]
        xb = x.astype(BF16)

        def ffn(wgu_ref, wd_ref):
            a = jnp.dot(xb, wgu_ref[0], preferred_element_type=F32)
            hid = _silu(a[:, :d_expert]) * a[:, d_expert:]
            return jnp.dot(hid.astype(BF16), wd_ref[0], preferred_element_type=F32)

        y = (ffn(wgu_lo_ref, wd_lo_ref) * gate_ref[:, 0:1]
             + ffn(wgu_hi_ref, wd_hi_ref) * gate_ref[:, 1:2])
        o_ref[...] = _layer_norm_rows(alpha * x + y, lng_ref[...], lnb_ref[...])


def _moe_ffn(x_sorted, gates_sorted, blk_lo, blk_hi, n_used, wgu, wd, ln_g, ln_b, alpha):
    n_slots, d = x_sorted.shape
    d_expert = wd.shape[1]
    n_blocks = n_slots // MOE_BLOCK
    grid_spec = pltpu.PrefetchScalarGridSpec(
        num_scalar_prefetch=3,
        grid=(n_blocks,),
        in_specs=[
            pl.BlockSpec((MOE_BLOCK, d), lambda i, lo, hi, nb: (i, 0)),
            pl.BlockSpec((MOE_BLOCK, 2), lambda i, lo, hi, nb: (i, 0)),
            pl.BlockSpec((1, d, 2 * d_expert), lambda i, lo, hi, nb: (lo[i], 0, 0)),
            pl.BlockSpec((1, d_expert, d), lambda i, lo, hi, nb: (lo[i], 0, 0)),
            pl.BlockSpec((1, d, 2 * d_expert), lambda i, lo, hi, nb: (hi[i], 0, 0)),
            pl.BlockSpec((1, d_expert, d), lambda i, lo, hi, nb: (hi[i], 0, 0)),
            pl.BlockSpec((1, d), lambda i, lo, hi, nb: (0, 0)),
            pl.BlockSpec((1, d), lambda i, lo, hi, nb: (0, 0)),
        ],
        out_specs=pl.BlockSpec((MOE_BLOCK, d), lambda i, lo, hi, nb: (i, 0)),
    )
    return pl.pallas_call(
        functools.partial(_moe_kernel, alpha=alpha, d_expert=d_expert),
        grid_spec=grid_spec,
        out_shape=jax.ShapeDtypeStruct((n_slots, d), F32),
        compiler_params=_cparams(("arbitrary",)),
        name="moe_ffn",
    )(blk_lo, blk_hi, n_used, x_sorted, gates_sorted, wgu, wd, wgu, wd,
      ln_g.reshape(1, d), ln_b.reshape(1, d))


def _dispatch_plan(route, n_tok):
    e1 = route[0].astype(jnp.int32)
    e2 = route[1].astype(jnp.int32)
    first_is_lo = e1 < e2
    lo = jnp.where(first_is_lo, e1, e2)
    hi = jnp.where(first_is_lo, e2, e1)
    g_lo = jnp.where(first_is_lo, route[2], route[3])
    g_hi = jnp.where(first_is_lo, route[3], route[2])
    grp = lo // EXPERTS_PER_GROUP
    a = lo % EXPERTS_PER_GROUP
    b = hi % EXPERTS_PER_GROUP
    cls = grp * PAIRS_PER_GROUP + (a * (7 - a)) // 2 + (b - a - 1)
    n_cls = N_GROUPS * PAIRS_PER_GROUP
    n_blocks = -(-(n_tok + n_cls * (MOE_BLOCK - 1)) // MOE_BLOCK)
    n_slots = n_blocks * MOE_BLOCK

    counts = jnp.bincount(cls, length=n_cls)
    start = jnp.cumsum(counts) - counts
    padded = (counts + MOE_BLOCK - 1) // MOE_BLOCK * MOE_BLOCK
    pend = jnp.cumsum(padded)
    pstart = pend - padded
    order = jnp.argsort(cls, stable=True).astype(jnp.int32)
    cls_s = cls[order]
    slot = (pstart[cls_s] + (jnp.arange(n_tok) - start[cls_s])).astype(jnp.int32)
    slot_tok = jnp.zeros((n_slots,), jnp.int32).at[slot].set(order)
    tok_slot = jnp.zeros((n_tok,), jnp.int32).at[order].set(slot)
    blk_cls = jnp.minimum(jnp.searchsorted(pend, jnp.arange(n_blocks) * MOE_BLOCK, side='right'),
                          n_cls - 1)
    pair_lo = jnp.array([0, 0, 0, 1, 1, 2], jnp.int32)
    pair_hi = jnp.array([1, 2, 3, 2, 3, 3], jnp.int32)
    blk_grp = blk_cls // PAIRS_PER_GROUP
    blk_pair = blk_cls % PAIRS_PER_GROUP
    blk_lo = (blk_grp * EXPERTS_PER_GROUP + pair_lo[blk_pair]).astype(jnp.int32)
    blk_hi = (blk_grp * EXPERTS_PER_GROUP + pair_hi[blk_pair]).astype(jnp.int32)
    n_used = (pend[-1] // MOE_BLOCK).astype(jnp.int32).reshape(1)
    gates = jnp.stack([g_lo, g_hi], axis=1)
    return slot_tok, tok_slot, blk_lo, blk_hi, n_used, gates


def _moe_layer(h, route, wgu, wd, ln_g, ln_b, alpha):
    n_tok = h.shape[0]
    slot_tok, tok_slot, blk_lo, blk_hi, n_used, gates = _dispatch_plan(route, n_tok)
    x_sorted = jnp.take(h, slot_tok, axis=0)
    gates_sorted = jnp.take(gates, slot_tok, axis=0)
    out_sorted = _moe_ffn(x_sorted, gates_sorted, blk_lo, blk_hi, n_used, wgu, wd,
                          ln_g, ln_b, alpha)
    return jnp.take(out_sorted, tok_slot, axis=0)


def _pad_cols(w, width):
    return jnp.pad(w, ((0, 0), (0, width - w.shape[1])))


def kernel(x, ln_mix_g, ln_mix_b, ln_ffn_g, ln_ffn_b, router_w, router_b, fox_w_in, fox_b_f,
           fox_q_gain, fox_k_gain, fox_w_out, gdn_w_in, gdn_conv_w, gdn_a_log, gdn_dt_bias,
           gdn_norm_g, gdn_w_out, moe_w_gate, moe_w_up, moe_w_down):
    bsz, seq, d = x.shape
    n = bsz * seq
    depth = ln_mix_g.shape[0]
    alpha = (2.0 * depth) ** 0.25
    fox_heads = fox_b_f.shape[1]
    fox_dim = d // fox_heads
    gdn_heads = gdn_a_log.shape[1]
    gdn_dim = d // gdn_heads
    rw_t = router_w.T

    h = x.reshape(n, d)
    for i in range(depth):
        j = i // 2
        if i % 2 == 0:
            w_in = fox_w_in[j]
            proj, side = _inproj(h, w_in[:, :4 * d].astype(BF16),
                                 _pad_cols(w_in[:, 4 * d:], LANES).astype(BF16))
            gains = jnp.concatenate([jnp.tile(fox_q_gain[j], fox_heads) * fox_dim ** -0.5,
                                     jnp.tile(fox_k_gain[j], fox_heads)]).reshape(1, 2 * d)
            qk3, v_t = _fox_prep(proj.reshape(bsz, seq, 4 * d), gains, fox_dim)
            f_t = side[:, :fox_heads].reshape(bsz, seq, fox_heads).transpose(0, 2, 1)
            c = _fox_gates(f_t, fox_b_f[j])
            c_row = c.reshape(bsz, fox_heads // 2, 2, seq)
            o = _fox_attention(qk3, v_t, c_row, fox_heads, fox_dim)
            h, route = _mix_out(o.reshape(n, d), proj, 3, h, fox_w_out[j].astype(BF16),
                                ln_mix_g[i], ln_mix_b[i], rw_t, router_b, alpha)
        else:
            w_in = gdn_w_in[j]
            proj, side = _inproj(h, w_in[:, :4 * d].astype(BF16),
                                 _pad_cols(w_in[:, 4 * d:], LANES).astype(BF16))
            qkv = _gdn_prep(proj.reshape(bsz, seq, 4 * d), gdn_conv_w[j], d, gdn_dim)
            side3 = side.reshape(bsz, seq, LANES)
            b_t = side3[:, :, :gdn_heads].transpose(0, 2, 1)
            a_t = side3[:, :, gdn_heads:2 * gdn_heads].transpose(0, 2, 1)
            beta, gc = _gdn_gates(b_t, a_t, gdn_a_log[j], gdn_dt_bias[j])
            rows = jnp.concatenate([beta, gc], axis=1).reshape(
                bsz, 2 * gdn_heads, seq // CHUNK, CHUNK).transpose(0, 2, 1, 3)
            cols = rows.transpose(0, 1, 3, 2)
            o = _gdn_delta_rule(qkv, cols, rows, gdn_heads, gdn_dim)
            h, route = _mix_out(o.reshape(n, d), proj, 3, h, gdn_w_out[j].astype(BF16),
                                ln_mix_g[i], ln_mix_b[i], rw_t, router_b, alpha,
                                norm_g=gdn_norm_g[j], gdn_head_dim=gdn_dim)
        wgu = jnp.concatenate([moe_w_gate[i], moe_w_up[i]], axis=2).astype(BF16)
        h = _moe_layer(h, route, wgu, moe_w_down[i].astype(BF16), ln_ffn_g[i], ln_ffn_b[i], alpha)
    return h.reshape(bsz, seq, d)
```

```python
import functools
import math

import jax
import jax.numpy as jnp
from jax import lax
from jax.experimental import pallas as pl
from jax.experimental.pallas import tpu as pltpu

F32 = jnp.float32
BF16 = jnp.bfloat16

N_GROUPS = 4
EXPERTS_PER_GROUP = 4
PAIRS_PER_GROUP = 6
CHUNK = 64
CONV_WIDTH = 4
LN_EPS = 1e-5
RMS_EPS = 1e-6
L2_EPS = 1e-6

LOG2E = math.log2(math.e)

LANES = 128
BF16_ROWS = 16
VMEM_LIMIT = 56 * 1024 * 1024

ROW_TILE = 512
ATT_Q_TILE = 256
ATT_K_TILE = 512
GDN_TILE = 256
MOE_BLOCK = 256
MOE_SPARE_ROWS = 2 * MOE_BLOCK


def _cparams(sem):
    return pltpu.CompilerParams(dimension_semantics=sem, vmem_limit_bytes=VMEM_LIMIT)


def _layer_norm_rows(y, g, b):
    mu = jnp.mean(y, axis=-1, keepdims=True)
    yc = y - mu
    var = jnp.mean(yc * yc, axis=-1, keepdims=True)
    return yc * lax.rsqrt(var + LN_EPS) * g + b


def _sigmoid(x):
    return 1.0 / (1.0 + jnp.exp(-x))


def _silu(x):
    return x * _sigmoid(x)


def _softplus(x):
    return jnp.maximum(x, 0.0) + jnp.log(1.0 + jnp.exp(-jnp.abs(x)))


def _inproj_kernel(x_ref, w_ref, ws_ref, o_ref, os_ref, *, col_chunk):
    xb = x_ref[...].astype(BF16)
    width = o_ref.shape[1]
    for j in range(width // col_chunk):
        sl = slice(j * col_chunk, (j + 1) * col_chunk)
        o_ref[:, sl] = jnp.dot(xb, w_ref[:, sl], preferred_element_type=F32).astype(BF16)
    os_ref[...] = jnp.dot(xb, ws_ref[...], preferred_element_type=F32)


def _inproj(x, n, w_main, w_side):
    d = x.shape[1]
    width = w_main.shape[1]
    tm = min(ROW_TILE, n)
    return pl.pallas_call(
        functools.partial(_inproj_kernel, col_chunk=1024),
        grid=(n // tm,),
        in_specs=[
            pl.BlockSpec((tm, d), lambda i: (i, 0)),
            pl.BlockSpec((d, width), lambda i: (0, 0)),
            pl.BlockSpec((d, LANES), lambda i: (0, 0)),
        ],
        out_specs=[
            pl.BlockSpec((tm, width), lambda i: (i, 0)),
            pl.BlockSpec((tm, LANES), lambda i: (i, 0)),
        ],
        out_shape=[
            jax.ShapeDtypeStruct((n, width), BF16),
            jax.ShapeDtypeStruct((n, LANES), F32),
        ],
        compiler_params=_cparams(("parallel",)),
        name="inproj",
    )(x, w_main, w_side)


def _fox_prep_kernel(p_ref, gain_ref, qk_ref, vt_ref, *, head_dim):
    lane = lax.broadcasted_iota(jnp.int32, (1, LANES), 1)
    first = lane < head_dim
    qk_width = qk_ref.shape[2]
    for j in range(qk_width // LANES):
        sl = slice(j * LANES, (j + 1) * LANES)
        x = p_ref[0, :, sl].astype(F32)
        xx = x * x
        s_a = jnp.sum(jnp.where(first, xx, 0.0), axis=-1, keepdims=True)
        s_b = jnp.sum(jnp.where(first, 0.0, xx), axis=-1, keepdims=True)
        r = jnp.where(first, lax.rsqrt(s_a / head_dim + RMS_EPS),
                      lax.rsqrt(s_b / head_dim + RMS_EPS))
        qk_ref[0, :, sl] = (x * r * gain_ref[:, sl]).astype(BF16)
    for j in range(vt_ref.shape[1] // LANES):
        v = p_ref[0, :, qk_width + j * LANES:qk_width + (j + 1) * LANES].astype(F32)
        vt_ref[0, j * LANES:(j + 1) * LANES, :] = v.T.astype(BF16)


def _fox_prep(proj3, gains, head_dim):
    bsz, seq, _ = proj3.shape
    qk_width = gains.shape[1]
    d = qk_width // 2
    ts = min(ROW_TILE, seq)
    return pl.pallas_call(
        functools.partial(_fox_prep_kernel, head_dim=head_dim),
        grid=(bsz, seq // ts),
        in_specs=[
            pl.BlockSpec((1, ts, qk_width + d), lambda b, i: (b, i, 0)),
            pl.BlockSpec((1, qk_width), lambda b, i: (0, 0)),
        ],
        out_specs=[
            pl.BlockSpec((1, ts, qk_width), lambda b, i: (b, i, 0)),
            pl.BlockSpec((1, d, ts), lambda b, i: (b, 0, i)),
        ],
        out_shape=[
            jax.ShapeDtypeStruct((bsz, seq, qk_width), BF16),
            jax.ShapeDtypeStruct((bsz, d, seq), BF16),
        ],
        compiler_params=_cparams(("parallel", "parallel")),
        name="fox_prep",
    )(proj3, gains)


def _prefix_sum_lanes(x, period):
    pos = lax.broadcasted_iota(jnp.int32, x.shape, 1) % period
    d = 1
    while d < period:
        x = x + jnp.where(pos >= d, pltpu.roll(x, d, axis=1), 0.0)
        d *= 2
    return x


def _fox_gate_kernel(f_ref, b_ref, c_ref):
    z = f_ref[0] + b_ref[...]
    log_f = -_softplus(-z)
    c_ref[0] = _prefix_sum_lanes(log_f, z.shape[1])


def _fox_gates(f_logit_t, b_f):
    bsz, h, s = f_logit_t.shape
    return pl.pallas_call(
        _fox_gate_kernel,
        grid=(bsz,),
        in_specs=[pl.BlockSpec((1, h, s), lambda b: (b, 0, 0)),
                  pl.BlockSpec((h, 1), lambda b: (0, 0))],
        out_specs=pl.BlockSpec((1, h, s), lambda b: (b, 0, 0)),
        out_shape=jax.ShapeDtypeStruct((bsz, h, s), F32),
        compiler_params=_cparams(("parallel",)),
        name="fox_gates",
    )(f_logit_t, b_f.reshape(h, 1))


def _gdn_gate_kernel(b_ref, a_ref, alog_ref, dt_ref, beta_ref, gc_ref):
    beta_ref[0] = _sigmoid(b_ref[0])
    g = -jnp.exp(alog_ref[...]) * _softplus(a_ref[0] + dt_ref[...])
    gc_ref[0] = _prefix_sum_lanes(g, CHUNK)


def _gdn_gates(b_logit_t, a_logit_t, a_log, dt_bias):
    bsz, h, s = b_logit_t.shape
    spec = pl.BlockSpec((1, h, s), lambda b: (b, 0, 0))
    vec = pl.BlockSpec((h, 1), lambda b: (0, 0))
    return pl.pallas_call(
        _gdn_gate_kernel,
        grid=(bsz,),
        in_specs=[spec, spec, vec, vec],
        out_specs=[spec, spec],
        out_shape=[jax.ShapeDtypeStruct((bsz, h, s), F32)] * 2,
        compiler_params=_cparams(("parallel",)),
        name="gdn_gates",
    )(b_logit_t, a_logit_t, a_log.reshape(h, 1), dt_bias.reshape(h, 1))


def _fox_attn_kernel(q_ref, k_ref, vt_ref, crow_ref, o_ref, ckb_ref, acc_ref, sbuf_ref,
                     *, head_dim, tq, tk):
    qi = pl.program_id(2)
    seq = k_ref.shape[1]

    @pl.when(qi == 0)
    def _():
        def fill(j, carry):
            st = pl.multiple_of(j * LANES, LANES)
            rows = crow_ref[0, 0, :, pl.ds(st, LANES)] * LOG2E
            for hh in range(2):
                ckb_ref[hh, pl.ds(st, LANES), :] = jnp.broadcast_to(
                    rows[hh:hh + 1, :], (LANES, LANES)).T
            return carry
        lax.fori_loop(0, seq // LANES, fill, 0)

    lane = lax.broadcasted_iota(jnp.int32, (1, LANES), 1)
    first = lane < head_dim
    q = q_ref[0]
    zero = jnp.zeros_like(q)
    q_heads = (jnp.where(first, q, zero), jnp.where(first, zero, q))
    q_start = pl.multiple_of(qi * tq, tq)
    c_q = crow_ref[0, 0, :, pl.ds(q_start, tq)] * LOG2E
    acc_ref[...] = jnp.zeros(acc_ref.shape, F32)
    ones = jnp.ones((acc_ref.shape[1] - head_dim, tk), BF16)

    def scores(j):
        start = pl.multiple_of(j * tk, tk)
        k = k_ref[0, pl.ds(start, tk), :]
        return [lax.dot_general(k, q_heads[hh], (((1,), (1,)), ((), ())),
                                preferred_element_type=F32) for hh in range(2)]

    def update(j, s_in, stats, masked):
        start = pl.multiple_of(j * tk, tk)
        out = []
        for hh in range(2):
            m_old = stats[hh]
            s0 = s_in[hh]
            cb = ckb_ref[hh, pl.ds(start, tk), :]
            slabs = []
            for t in range(tq // LANES):
                sl = s0[:, t * LANES:(t + 1) * LANES] - cb
                if masked:
                    kpos = start + lax.broadcasted_iota(jnp.int32, sl.shape, 0)
                    qpos = q_start + t * LANES + lax.broadcasted_iota(jnp.int32, sl.shape, 1)
                    sl = jnp.where(kpos <= qpos, sl, -jnp.inf)
                slabs.append(sl)
            s = jnp.concatenate(slabs, axis=1)
            cq = c_q[hh:hh + 1, :]
            m_new = jnp.maximum(m_old, jnp.max(s, axis=0, keepdims=True) + cq)
            p = jnp.exp2(s + (cq - m_new))
            alpha = jnp.exp2(m_old - m_new)
            vt = jnp.concatenate(
                [vt_ref[0, hh * head_dim:(hh + 1) * head_dim, pl.ds(start, tk)], ones], axis=0)
            pv = jnp.dot(vt, p.astype(BF16), preferred_element_type=F32)
            acc_ref[hh] = acc_ref[hh] * alpha + pv
            out.append(m_new)
        return tuple(out)

    def park(s):
        sbuf_ref[0] = s[0]
        sbuf_ref[1] = s[1]

    def parked():
        return [sbuf_ref[0], sbuf_ref[1]]

    def pair(i, stats):
        j0 = 2 * i
        s1 = scores(j0 + 1)
        stats = update(j0, parked(), stats, False)
        park(scores(j0 + 2))
        return update(j0 + 1, s1, stats, False)

    neg = jnp.full((1, tq), -1e30, F32)
    n_full = (qi * tq) // tk
    n_pairs = n_full // 2
    park(scores(0))
    stats = lax.fori_loop(0, n_pairs, pair, (neg, neg))
    j0 = 2 * n_pairs

    def tail_two(stats):
        s1 = scores(j0 + 1)
        stats = update(j0, parked(), stats, False)
        return update(j0 + 1, s1, stats, True)

    def tail_one(stats):
        return update(j0, parked(), stats, True)

    lax.cond(n_full - j0 == 1, tail_two, tail_one, stats)
    o_t = jnp.concatenate(
        [acc_ref[hh, :head_dim, :] / acc_ref[hh, head_dim:head_dim + 1, :] for hh in range(2)],
        axis=0)
    o_ref[0] = o_t.T.astype(BF16)


def _fox_attention(qk3, v_t, c_row, n_heads, head_dim):
    bsz, seq, _ = qk3.shape
    d = n_heads * head_dim
    pairs = d // LANES
    tq = min(ATT_Q_TILE, seq)
    tk = min(ATT_K_TILE, seq)
    assert tk % tq == 0 and seq % tk == 0
    return pl.pallas_call(
        functools.partial(_fox_attn_kernel, head_dim=head_dim, tq=tq, tk=tk),
        grid=(bsz, pairs, seq // tq),
        in_specs=[
            pl.BlockSpec((1, tq, LANES), lambda b, p, i: (b, i, p)),
            pl.BlockSpec((1, seq, LANES), lambda b, p, i: (b, 0, pairs + p)),
            pl.BlockSpec((1, LANES, seq), lambda b, p, i: (b, p, 0)),
            pl.BlockSpec((1, 1, 2, seq), lambda b, p, i: (b, p, 0, 0)),
        ],
        out_specs=pl.BlockSpec((1, tq, LANES), lambda b, p, i: (b, i, p)),
        out_shape=jax.ShapeDtypeStruct((bsz, seq, d), BF16),
        scratch_shapes=[
            pltpu.VMEM((2, seq, LANES), F32),
            pltpu.VMEM((2, head_dim + BF16_ROWS, tq), F32),
            pltpu.VMEM((2, tk, tq), F32),
        ],
        compiler_params=_cparams(("parallel", "parallel", "arbitrary")),
        name="fox_attention",
    )(qk3, qk3, v_t, c_row)


def _gdn_prep_kernel(p_ref, w_ref, o_ref, buf_ref, *, d_model, head_dim, q_scale):
    ts = p_ref.shape[1]
    pad = 8

    @pl.when(pl.program_id(1) == 0)
    def _():
        buf_ref[0:pad, :] = jnp.zeros((pad, buf_ref.shape[1]), F32)

    buf_ref[pad:pad + ts, :] = p_ref[0].astype(F32)
    for j in range(o_ref.shape[2] // LANES):
        sl = slice(j * LANES, (j + 1) * LANES)
        acc = buf_ref[pad:pad + ts, sl] * w_ref[CONV_WIDTH - 1:CONV_WIDTH, sl]
        for dlt in range(1, CONV_WIDTH):
            acc = acc + (buf_ref[pad - dlt:pad - dlt + ts, sl]
                         * w_ref[CONV_WIDTH - 1 - dlt:CONV_WIDTH - dlt, sl])
        y = _silu(acc)
        if j * LANES < 2 * d_model:
            y = y * lax.rsqrt(jnp.sum(y * y, axis=-1, keepdims=True) + L2_EPS)
            if j * LANES < d_model:
                y = y * q_scale
        o_ref[0, :, sl] = y.astype(BF16)
    buf_ref[0:pad, :] = buf_ref[ts:ts + pad, :]


def _gdn_prep(proj3, conv_w, d_model, head_dim):
    bsz, seq, _ = proj3.shape
    width = 3 * d_model
    ts = min(ROW_TILE, seq)
    return pl.pallas_call(
        functools.partial(_gdn_prep_kernel, d_model=d_model, head_dim=head_dim,
                          q_scale=head_dim ** -0.5),
        grid=(bsz, seq // ts),
        in_specs=[
            pl.BlockSpec((1, ts, width), lambda b, i: (b, i, 0)),
            pl.BlockSpec((CONV_WIDTH, width), lambda b, i: (0, 0)),
        ],
        out_specs=pl.BlockSpec((1, ts, width), lambda b, i: (b, i, 0)),
        out_shape=jax.ShapeDtypeStruct((bsz, seq, width), BF16),
        scratch_shapes=[pltpu.VMEM((ts + 8, width), F32)],
        compiler_params=_cparams(("parallel", "arbitrary")),
        name="gdn_prep",
    )(proj3, conv_w)


def _gdn_chunk_kernel(q_ref, k_ref, v_ref, col_ref, row_ref, o_ref, state_ref,
                      *, n_heads, head_dim):
    @pl.when(pl.program_id(1) == 0)
    def _():
        state_ref[...] = jnp.zeros(state_ref.shape, F32)

    n_chunks = q_ref.shape[1] // CHUNK
    ri = lax.broadcasted_iota(jnp.int32, (CHUNK, CHUNK), 0)
    ci = lax.broadcasted_iota(jnp.int32, (CHUNK, CHUNK), 1)
    eye = jnp.where(ri == ci, 1.0, 0.0)
    heads = range(n_heads)
    nt = (((1,), (1,)), ((), ()))

    def mm(a, b):
        return jnp.dot(a, b, preferred_element_type=F32)

    def stack(a, b):
        return jnp.concatenate([a, b], axis=0)

    def chunk(c, carry):
        start = pl.multiple_of(c * CHUNK, CHUNK)
        cols = col_ref[0, c]
        rows = row_ref[0, c]
        sls = [slice(h * head_dim, (h + 1) * head_dim) for h in heads]
        q = [q_ref[0, pl.ds(start, CHUNK), sls[h]] for h in heads]
        k = [k_ref[0, pl.ds(start, CHUNK), sls[h]] for h in heads]
        v = [v_ref[0, pl.ds(start, CHUNK), sls[h]] for h in heads]
        beta = [cols[:, h:h + 1] for h in heads]
        g_col = [cols[:, n_heads + h:n_heads + h + 1] for h in heads]
        g_row = [rows[n_heads + h:n_heads + h + 1, :] for h in heads]
        g_last = [g_col[h][CHUNK - 1:CHUNK, :] for h in heads]
        decay = [jnp.exp(jnp.where(ri >= ci, g_col[h] - g_row[h], -jnp.inf)) for h in heads]
        kf = [k[h].astype(F32) for h in heads]
        kb = [kf[h] * beta[h] for h in heads]

        r1 = [lax.dot_general(stack(kb[h].astype(BF16), q[h]), k[h], nt,
                              preferred_element_type=F32) for h in heads]
        m = [jnp.where(ri > ci, r1[h][:CHUNK] * decay[h], 0.0) for h in heads]
        attn = [(r1[h][CHUNK:] * decay[h]).astype(BF16) for h in heads]

        t = [eye - m[h] for h in heads]
        pb = [m[h].astype(BF16) for h in heads]
        p = [mm(pb[h], pb[h]) for h in heads]
        power = 2
        while power < CHUNK:
            pb = [p[h].astype(BF16) for h in heads]
            if 2 * power >= CHUNK:
                t = [t[h] + mm(t[h].astype(BF16), pb[h]) for h in heads]
            else:
                r = [mm(stack(t[h].astype(BF16), pb[h]), pb[h]) for h in heads]
                t = [t[h] + r[h][:CHUNK] for h in heads]
                p = [r[h][CHUNK:] for h in heads]
            power *= 2

        e_g = [jnp.exp(g_col[h]) for h in heads]
        rhs = [jnp.concatenate([v[h].astype(F32) * beta[h], kb[h] * e_g[h]], axis=1).astype(BF16)
               for h in heads]
        sol = [mm(t[h].astype(BF16), rhs[h]) for h in heads]
        q_dec = [(q[h].astype(F32) * e_g[h]).astype(BF16) for h in heads]
        k_dec_t = [(kf[h] * jnp.exp(g_last[h] - g_col[h])).T.astype(BF16) for h in heads]

        sb = [state_ref[h].astype(BF16) for h in heads]
        r2 = [mm(stack(sol[h][:, head_dim:].astype(BF16), q_dec[h]), sb[h]) for h in heads]
        vb = [(sol[h][:, :head_dim] - r2[h][:CHUNK]).astype(BF16) for h in heads]
        r3 = [mm(stack(attn[h], k_dec_t[h]), vb[h]) for h in heads]
        for h in heads:
            o_ref[0, pl.ds(start, CHUNK), sls[h]] = (r2[h][CHUNK:] + r3[h][:CHUNK]).astype(BF16)
            state_ref[h] = state_ref[h] * jnp.exp(g_last[h]) + r3[h][CHUNK:]
        return carry

    lax.fori_loop(0, n_chunks, chunk, 0)


def _gdn_delta_rule(qkv3, cols, rows, n_heads, head_dim):
    bsz, seq, _ = qkv3.shape
    d = n_heads * head_dim
    ts = min(GDN_TILE, seq)
    cpt = ts // CHUNK
    return pl.pallas_call(
        functools.partial(_gdn_chunk_kernel, n_heads=n_heads, head_dim=head_dim),
        grid=(bsz, seq // ts),
        in_specs=[
            pl.BlockSpec((1, ts, d), lambda b, i: (b, i, 0)),
            pl.BlockSpec((1, ts, d), lambda b, i: (b, i, 1)),
            pl.BlockSpec((1, ts, d), lambda b, i: (b, i, 2)),
            pl.BlockSpec((1, cpt, CHUNK, 2 * n_heads), lambda b, i: (b, i, 0, 0)),
            pl.BlockSpec((1, cpt, 2 * n_heads, CHUNK), lambda b, i: (b, i, 0, 0)),
        ],
        out_specs=pl.BlockSpec((1, ts, d), lambda b, i: (b, i, 0)),
        out_shape=jax.ShapeDtypeStruct((bsz, seq, d), BF16),
        scratch_shapes=[pltpu.VMEM((n_heads, head_dim, head_dim), F32)],
        compiler_params=_cparams(("parallel", "arbitrary")),
        name="gdn_delta_rule",
    )(qkv3, qkv3, qkv3, cols, rows)


def _route_rows(logits_t, bias):
    n_exp = logits_t.shape[0]
    mx = jnp.max(logits_t, axis=0, keepdims=True)
    ex = jnp.exp(logits_t - mx)
    probs = ex / jnp.sum(ex, axis=0, keepdims=True)
    sel = probs + bias
    sel_r = [sel[e:e + 1, :] for e in range(n_exp)]
    prob_r = [probs[e:e + 1, :] for e in range(n_exp)]

    def top2_sum(a, b, c, d):
        hi1, lo1 = jnp.maximum(a, b), jnp.minimum(a, b)
        hi2, lo2 = jnp.maximum(c, d), jnp.minimum(c, d)
        return jnp.maximum(hi1, hi2) + jnp.maximum(jnp.minimum(hi1, hi2), jnp.maximum(lo1, lo2))

    scores = [top2_sum(*sel_r[EXPERTS_PER_GROUP * g:EXPERTS_PER_GROUP * (g + 1)])
              for g in range(N_GROUPS)]
    best = scores[0]
    gidx = jnp.zeros(best.shape, jnp.int32)
    for g in range(1, N_GROUPS):
        upd = scores[g] > best
        best = jnp.where(upd, scores[g], best)
        gidx = jnp.where(upd, g, gidx)

    def pick(vals, j):
        out = vals[j]
        for g in range(1, N_GROUPS):
            out = jnp.where(gidx == g, vals[EXPERTS_PER_GROUP * g + j], out)
        return out

    cand = [pick(sel_r, j) for j in range(EXPERTS_PER_GROUP)]
    cprob = [pick(prob_r, j) for j in range(EXPERTS_PER_GROUP)]
    b1, i1, p1 = cand[0], jnp.zeros(best.shape, jnp.int32), cprob[0]
    for j in range(1, EXPERTS_PER_GROUP):
        upd = cand[j] > b1
        b1 = jnp.where(upd, cand[j], b1)
        i1 = jnp.where(upd, j, i1)
        p1 = jnp.where(upd, cprob[j], p1)
    b2 = jnp.full(best.shape, -jnp.inf, F32)
    i2 = jnp.full(best.shape, -1, jnp.int32)
    p2 = jnp.zeros(best.shape, F32)
    for j in range(EXPERTS_PER_GROUP):
        upd = jnp.logical_and(i1 != j, jnp.logical_or(cand[j] > b2, i2 < 0))
        b2 = jnp.where(upd, cand[j], b2)
        i2 = jnp.where(upd, j, i2)
        p2 = jnp.where(upd, cprob[j], p2)
    tot = p1 + p2
    base = gidx * EXPERTS_PER_GROUP
    return base + i1, base + i2, p1 / tot, p2 / tot


def _mix_out_kernel(o_ref, og_ref, h_ref, w_ref, lng_ref, lnb_ref, rw_ref, rb_ref,
                    *rest, alpha, gdn_head_dim):
    if gdn_head_dim:
        ng_ref, hn_ref, route_ref = rest
    else:
        hn_ref, route_ref = rest
    o = o_ref[...].astype(F32)
    og = og_ref[...].astype(F32)
    if gdn_head_dim:
        parts = []
        for j in range(o.shape[1] // gdn_head_dim):
            x = o[:, j * gdn_head_dim:(j + 1) * gdn_head_dim]
            r = lax.rsqrt(jnp.mean(x * x, axis=-1, keepdims=True) + RMS_EPS)
            parts.append(x * r * ng_ref[...])
        a = jnp.concatenate(parts, axis=1) * _silu(og)
    else:
        a = o * _sigmoid(og)
    mix = jnp.dot(a.astype(BF16), w_ref[...], preferred_element_type=F32)
    hn = _layer_norm_rows(alpha * h_ref[...] + mix, lng_ref[...], lnb_ref[...])
    hn_ref[...] = hn
    logits_t = lax.dot_general(rw_ref[...], hn, (((1,), (1,)), ((), ())),
                               preferred_element_type=F32,
                               precision=lax.Precision.HIGHEST)
    e1, e2, g1, g2 = _route_rows(logits_t, rb_ref[...])
    route_ref[0:1, :] = e1.astype(F32)
    route_ref[1:2, :] = e2.astype(F32)
    route_ref[2:3, :] = g1
    route_ref[3:4, :] = g2
    route_ref[4:8, :] = jnp.zeros((4, g1.shape[1]), F32)


def _mix_out(o, proj, og_col_block, h, w_out, ln_g, ln_b, rw_t, rb, alpha,
             norm_g=None, gdn_head_dim=0):
    n, d = o.shape
    n_exp = rw_t.shape[0]
    tm = min(ROW_TILE, n)
    row = lambda i: (i, 0)
    const = lambda i: (0, 0)
    in_specs = [
        pl.BlockSpec((tm, d), row),
        pl.BlockSpec((tm, d), lambda i: (i, og_col_block)),
        pl.BlockSpec((tm, d), row),
        pl.BlockSpec((d, d), const),
        pl.BlockSpec((1, d), const),
        pl.BlockSpec((1, d), const),
        pl.BlockSpec((n_exp, d), const),
        pl.BlockSpec((n_exp, 1), const),
    ]
    args = [o, proj, h, w_out, ln_g.reshape(1, d), ln_b.reshape(1, d), rw_t,
            rb.reshape(n_exp, 1)]
    if gdn_head_dim:
        in_specs.append(pl.BlockSpec((1, gdn_head_dim), const))
        args.append(norm_g.reshape(1, gdn_head_dim))
    return pl.pallas_call(
        functools.partial(_mix_out_kernel, alpha=alpha, gdn_head_dim=gdn_head_dim),
        grid=(n // tm,),
        in_specs=in_specs,
        out_specs=[pl.BlockSpec((tm, d), row), pl.BlockSpec((8, tm), lambda i: (0, i))],
        out_shape=[jax.ShapeDtypeStruct((n, d), F32), jax.ShapeDtypeStruct((8, n), F32)],
        compiler_params=_cparams(("parallel",)),
        name="mix_out",
    )(*args)


def _moe_kernel(elo_ref, ehi_ref, nblk_ref, src_ref, src_next_ref, dst_ref, gate_ref, h_hbm,
                wgu_lo_ref, wd_lo_ref, wgu_hi_ref, wd_hi_ref, lng_ref, lnb_ref, out_hbm,
                xbuf, ybuf, gsem, ssem, *, alpha, d_expert, n_tok):
    del elo_ref, ehi_ref
    i = pl.program_id(0)
    nblk = nblk_ref[0]
    slot = lax.rem(i, 2)
    blk = xbuf.shape[1]

    def gather(idx_ref, dst):
        for r in range(blk):
            pltpu.make_async_copy(h_hbm.at[pl.ds(idx_ref[0, 0, r], 1)],
                                  xbuf.at[dst, pl.ds(r, 1)], gsem.at[dst]).start()

    def wait_gather(dst):
        pltpu.make_async_copy(h_hbm.at[pl.ds(0, blk)], xbuf.at[dst], gsem.at[dst]).wait()

    def wait_scatter(src):
        pltpu.make_async_copy(ybuf.at[src], out_hbm.at[pl.ds(0, blk)], ssem.at[src]).wait()

    @pl.when(i == 0)
    def _():
        ybuf[...] = jnp.zeros(ybuf.shape, F32)
        spare = [pltpu.make_async_copy(ybuf.at[s], out_hbm.at[pl.ds(n_tok + s * blk, blk)],
                                       ssem.at[s]) for s in range(2)]
        for cp in spare:
            cp.start()
        for cp in spare:
            cp.wait()
        gather(src_ref, 0)

    @pl.when(i + 1 < nblk)
    def _():
        gather(src_next_ref, 1 - slot)

    @pl.when(i < nblk)
    def _():
        wait_gather(slot)
        x = xbuf[slot]
        xb = x.astype(BF16)

        def ffn(wgu_ref, wd_ref):
            a = jnp.dot(xb, wgu_ref[0], preferred_element_type=F32)
            hid = _silu(a[:, :d_expert]) * a[:, d_expert:]
            return jnp.dot(hid.astype(BF16), wd_ref[0], preferred_element_type=F32)

        y = (ffn(wgu_lo_ref, wd_lo_ref) * gate_ref[:, 0:1]
             + ffn(wgu_hi_ref, wd_hi_ref) * gate_ref[:, 1:2])
        res = _layer_norm_rows(alpha * x + y, lng_ref[...], lnb_ref[...])

        @pl.when(i >= 2)
        def _():
            wait_scatter(slot)

        ybuf[slot] = res
        for r in range(blk):
            pltpu.make_async_copy(ybuf.at[slot, pl.ds(r, 1)],
                                  out_hbm.at[pl.ds(dst_ref[0, 0, r], 1)], ssem.at[slot]).start()

        @pl.when(i == nblk - 1)
        def _():
            wait_scatter(slot)

            @pl.when(i >= 1)
            def _():
                wait_scatter(1 - slot)


def _moe_ffn(h, n_tok, src_idx, dst_idx, gates_sorted, blk_lo, blk_hi, n_used, wgu, wd,
             ln_g, ln_b, alpha):
    d = h.shape[1]
    d_expert = wd.shape[1]
    n_blocks = src_idx.shape[0] // MOE_BLOCK
    src = src_idx.reshape(n_blocks, 1, MOE_BLOCK)
    dst = dst_idx.reshape(n_blocks, 1, MOE_BLOCK)
    last = n_blocks - 1
    smem_blk = functools.partial(pl.BlockSpec, (1, 1, MOE_BLOCK), memory_space=pltpu.SMEM)
    grid_spec = pltpu.PrefetchScalarGridSpec(
        num_scalar_prefetch=3,
        grid=(n_blocks,),
        in_specs=[
            smem_blk(lambda i, lo, hi, nb: (i, 0, 0)),
            smem_blk(lambda i, lo, hi, nb: (jnp.minimum(i + 1, last), 0, 0)),
            smem_blk(lambda i, lo, hi, nb: (i, 0, 0)),
            pl.BlockSpec((MOE_BLOCK, 2), lambda i, lo, hi, nb: (i, 0)),
            pl.BlockSpec(memory_space=pl.ANY),
            pl.BlockSpec((1, d, 2 * d_expert), lambda i, lo, hi, nb: (lo[i], 0, 0)),
            pl.BlockSpec((1, d_expert, d), lambda i, lo, hi, nb: (lo[i], 0, 0)),
            pl.BlockSpec((1, d, 2 * d_expert), lambda i, lo, hi, nb: (hi[i], 0, 0)),
            pl.BlockSpec((1, d_expert, d), lambda i, lo, hi, nb: (hi[i], 0, 0)),
            pl.BlockSpec((1, d), lambda i, lo, hi, nb: (0, 0)),
            pl.BlockSpec((1, d), lambda i, lo, hi, nb: (0, 0)),
        ],
        out_specs=pl.BlockSpec(memory_space=pl.ANY),
        scratch_shapes=[
            pltpu.VMEM((2, MOE_BLOCK, d), F32),
            pltpu.VMEM((2, MOE_BLOCK, d), F32),
            pltpu.SemaphoreType.DMA((2,)),
            pltpu.SemaphoreType.DMA((2,)),
        ],
    )
    return pl.pallas_call(
        functools.partial(_moe_kernel, alpha=alpha, d_expert=d_expert, n_tok=n_tok),
        grid_spec=grid_spec,
        out_shape=jax.ShapeDtypeStruct((n_tok + MOE_SPARE_ROWS, d), F32),
        compiler_params=_cparams(("arbitrary",)),
        name="moe_ffn",
    )(blk_lo, blk_hi, n_used, src, src, dst, gates_sorted, h, wgu, wd, wgu, wd,
      ln_g.reshape(1, d), ln_b.reshape(1, d))


def _dispatch_plan(route, n_tok):
    e1 = route[0].astype(jnp.int32)
    e2 = route[1].astype(jnp.int32)
    first_is_lo = e1 < e2
    lo = jnp.where(first_is_lo, e1, e2)
    hi = jnp.where(first_is_lo, e2, e1)
    g_lo = jnp.where(first_is_lo, route[2], route[3])
    g_hi = jnp.where(first_is_lo, route[3], route[2])
    grp = lo // EXPERTS_PER_GROUP
    a = lo % EXPERTS_PER_GROUP
    b = hi % EXPERTS_PER_GROUP
    cls = grp * PAIRS_PER_GROUP + (a * (7 - a)) // 2 + (b - a - 1)
    n_cls = N_GROUPS * PAIRS_PER_GROUP
    n_blocks = -(-(n_tok + n_cls * (MOE_BLOCK - 1)) // MOE_BLOCK)
    n_slots = n_blocks * MOE_BLOCK

    counts = jnp.bincount(cls, length=n_cls)
    start = jnp.cumsum(counts) - counts
    padded = (counts + MOE_BLOCK - 1) // MOE_BLOCK * MOE_BLOCK
    pend = jnp.cumsum(padded)
    pstart = pend - padded
    order = jnp.argsort(cls, stable=True).astype(jnp.int32)
    slot_ids = jnp.arange(n_slots)
    slot_cls = jnp.minimum(jnp.searchsorted(pend, slot_ids, side='right'), n_cls - 1)
    rank = slot_ids - pstart[slot_cls]
    valid = rank < counts[slot_cls]
    slot_tok = jnp.where(valid, order[jnp.clip(start[slot_cls] + rank, 0, n_tok - 1)], 0)
    slot_tok = slot_tok.astype(jnp.int32)
    spare = n_tok + (slot_ids // MOE_BLOCK % 2) * MOE_BLOCK + slot_ids % MOE_BLOCK
    slot_dst = jnp.where(valid, slot_tok, spare).astype(jnp.int32)
    blk_cls = slot_cls[::MOE_BLOCK]
    pair_lo = jnp.array([0, 0, 0, 1, 1, 2], jnp.int32)
    pair_hi = jnp.array([1, 2, 3, 2, 3, 3], jnp.int32)
    blk_grp = blk_cls // PAIRS_PER_GROUP
    blk_pair = blk_cls % PAIRS_PER_GROUP
    blk_lo = (blk_grp * EXPERTS_PER_GROUP + pair_lo[blk_pair]).astype(jnp.int32)
    blk_hi = (blk_grp * EXPERTS_PER_GROUP + pair_hi[blk_pair]).astype(jnp.int32)
    n_used = (pend[-1] // MOE_BLOCK).astype(jnp.int32).reshape(1)
    gates_sorted = jnp.stack([g_lo[slot_tok], g_hi[slot_tok]], axis=1)
    return slot_tok, slot_dst, gates_sorted, blk_lo, blk_hi, n_used


def _moe_layer(h, route, wgu, wd, ln_g, ln_b, alpha):
    n_tok = h.shape[0]
    slot_tok, slot_dst, gates_sorted, blk_lo, blk_hi, n_used = _dispatch_plan(route, n_tok)
    return _moe_ffn(h, n_tok, slot_tok, slot_dst, gates_sorted, blk_lo, blk_hi, n_used, wgu, wd,
                    ln_g, ln_b, alpha)


def _pad_cols(w, width):
    return jnp.pad(w, ((0, 0), (0, width - w.shape[1])))


def kernel(x, ln_mix_g, ln_mix_b, ln_ffn_g, ln_ffn_b, router_w, router_b, fox_w_in, fox_b_f,
           fox_q_gain, fox_k_gain, fox_w_out, gdn_w_in, gdn_conv_w, gdn_a_log, gdn_dt_bias,
           gdn_norm_g, gdn_w_out, moe_w_gate, moe_w_up, moe_w_down):
    bsz, seq, d = x.shape
    n = bsz * seq
    depth = ln_mix_g.shape[0]
    alpha = (2.0 * depth) ** 0.25
    fox_heads = fox_b_f.shape[1]
    fox_dim = d // fox_heads
    gdn_heads = gdn_a_log.shape[1]
    gdn_dim = d // gdn_heads
    rw_t = router_w.T

    h = x.reshape(n, d)
    for i in range(depth):
        j = i // 2
        if i % 2 == 0:
            w_in = fox_w_in[j]
            proj, side = _inproj(h, n, w_in[:, :4 * d].astype(BF16),
                                 _pad_cols(w_in[:, 4 * d:], LANES).astype(BF16))
            gains = jnp.concatenate([jnp.tile(fox_q_gain[j], fox_heads) * (fox_dim ** -0.5 * LOG2E),
                                     jnp.tile(fox_k_gain[j], fox_heads)]).reshape(1, 2 * d)
            qk3, v_t = _fox_prep(proj.reshape(bsz, seq, 4 * d), gains, fox_dim)
            f_t = side[:, :fox_heads].reshape(bsz, seq, fox_heads).transpose(0, 2, 1)
            c = _fox_gates(f_t, fox_b_f[j])
            c_row = c.reshape(bsz, fox_heads // 2, 2, seq)
            o = _fox_attention(qk3, v_t, c_row, fox_heads, fox_dim)
            h, route = _mix_out(o.reshape(n, d), proj, 3, h, fox_w_out[j].astype(BF16),
                                ln_mix_g[i], ln_mix_b[i], rw_t, router_b, alpha)
        else:
            w_in = gdn_w_in[j]
            proj, side = _inproj(h, n, w_in[:, :4 * d].astype(BF16),
                                 _pad_cols(w_in[:, 4 * d:], LANES).astype(BF16))
            qkv = _gdn_prep(proj.reshape(bsz, seq, 4 * d), gdn_conv_w[j], d, gdn_dim)
            side3 = side.reshape(bsz, seq, LANES)
            b_t = side3[:, :, :gdn_heads].transpose(0, 2, 1)
            a_t = side3[:, :, gdn_heads:2 * gdn_heads].transpose(0, 2, 1)
            beta, gc = _gdn_gates(b_t, a_t, gdn_a_log[j], gdn_dt_bias[j])
            rows = jnp.concatenate([beta, gc], axis=1).reshape(
                bsz, 2 * gdn_heads, seq // CHUNK, CHUNK).transpose(0, 2, 1, 3)
            cols = rows.transpose(0, 1, 3, 2)
            o = _gdn_delta_rule(qkv, cols, rows, gdn_heads, gdn_dim)
            h, route = _mix_out(o.reshape(n, d), proj, 3, h, gdn_w_out[j].astype(BF16),
                                ln_mix_g[i], ln_mix_b[i], rw_t, router_b, alpha,
                                norm_g=gdn_norm_g[j], gdn_head_dim=gdn_dim)
        wgu = jnp.concatenate([moe_w_gate[i], moe_w_up[i]], axis=2).astype(BF16)
        h = _moe_layer(h, route, wgu, moe_w_down[i].astype(BF16), ln_ffn_g[i], ln_ffn_b[i], alpha)
    return h[:n].reshape(bsz, seq, d)
```

```python
import functools
import math

import jax
import jax.numpy as jnp
from jax import lax
from jax.experimental import pallas as pl
from jax.experimental.pallas import tpu as pltpu

F32 = jnp.float32
BF16 = jnp.bfloat16

N_GROUPS = 4
EXPERTS_PER_GROUP = 4
PAIRS_PER_GROUP = 6
CHUNK = 64
CONV_WIDTH = 4
LN_EPS = 1e-5
RMS_EPS = 1e-6
L2_EPS = 1e-6

LOG2E = math.log2(math.e)

LANES = 128
BF16_ROWS = 16
VMEM_LIMIT = 56 * 1024 * 1024

ROW_TILE = 512
ATT_Q_TILE = 512
GDN_TILE = 256
GDN_UNROLL = 2
MOE_BLOCK = 256
MOE_SPARE_ROWS = 2 * MOE_BLOCK


def _cparams(sem):
    return pltpu.CompilerParams(dimension_semantics=sem, vmem_limit_bytes=VMEM_LIMIT)


def _layer_norm_rows(y, g, b):
    mu = jnp.mean(y, axis=-1, keepdims=True)
    yc = y - mu
    var = jnp.mean(yc * yc, axis=-1, keepdims=True)
    return yc * lax.rsqrt(var + LN_EPS) * g + b


def _sigmoid(x):
    return 1.0 / (1.0 + jnp.exp(-x))


def _silu(x):
    return x * _sigmoid(x)


def _softplus(x):
    return jnp.maximum(x, 0.0) + jnp.log(1.0 + jnp.exp(-jnp.abs(x)))


def _inproj_kernel(x_ref, w_ref, ws_ref, o_ref, os_ref, *, col_chunk):
    xb = x_ref[...].astype(BF16)
    width = o_ref.shape[1]
    for j in range(width // col_chunk):
        sl = slice(j * col_chunk, (j + 1) * col_chunk)
        o_ref[:, sl] = jnp.dot(xb, w_ref[:, sl], preferred_element_type=F32).astype(BF16)
    os_ref[...] = jnp.dot(xb, ws_ref[...], preferred_element_type=F32)


def _inproj(x, n, w_main, w_side):
    d = x.shape[1]
    width = w_main.shape[1]
    tm = min(ROW_TILE, n)
    return pl.pallas_call(
        functools.partial(_inproj_kernel, col_chunk=1024),
        grid=(n // tm,),
        in_specs=[
            pl.BlockSpec((tm, d), lambda i: (i, 0)),
            pl.BlockSpec((d, width), lambda i: (0, 0)),
            pl.BlockSpec((d, LANES), lambda i: (0, 0)),
        ],
        out_specs=[
            pl.BlockSpec((tm, width), lambda i: (i, 0)),
            pl.BlockSpec((tm, LANES), lambda i: (i, 0)),
        ],
        out_shape=[
            jax.ShapeDtypeStruct((n, width), BF16),
            jax.ShapeDtypeStruct((n, LANES), F32),
        ],
        compiler_params=_cparams(("parallel",)),
        name="inproj",
    )(x, w_main, w_side)


def _fox_prep_kernel(p_ref, gain_ref, qk_ref, vt_ref, *, head_dim):
    lane = lax.broadcasted_iota(jnp.int32, (1, LANES), 1)
    first = lane < head_dim
    qk_width = qk_ref.shape[2]
    for j in range(qk_width // LANES):
        sl = slice(j * LANES, (j + 1) * LANES)
        x = p_ref[0, :, sl].astype(F32)
        xx = x * x
        s_a = jnp.sum(jnp.where(first, xx, 0.0), axis=-1, keepdims=True)
        s_b = jnp.sum(jnp.where(first, 0.0, xx), axis=-1, keepdims=True)
        r = jnp.where(first, lax.rsqrt(s_a / head_dim + RMS_EPS),
                      lax.rsqrt(s_b / head_dim + RMS_EPS))
        qk_ref[0, :, sl] = (x * r * gain_ref[:, sl]).astype(BF16)
    for j in range(vt_ref.shape[1] // LANES):
        v = p_ref[0, :, qk_width + j * LANES:qk_width + (j + 1) * LANES].astype(F32)
        vt_ref[0, j * LANES:(j + 1) * LANES, :] = v.T.astype(BF16)


def _fox_prep(proj3, gains, head_dim):
    bsz, seq, _ = proj3.shape
    qk_width = gains.shape[1]
    d = qk_width // 2
    ts = min(ROW_TILE, seq)
    return pl.pallas_call(
        functools.partial(_fox_prep_kernel, head_dim=head_dim),
        grid=(bsz, seq // ts),
        in_specs=[
            pl.BlockSpec((1, ts, qk_width + d), lambda b, i: (b, i, 0)),
            pl.BlockSpec((1, qk_width), lambda b, i: (0, 0)),
        ],
        out_specs=[
            pl.BlockSpec((1, ts, qk_width), lambda b, i: (b, i, 0)),
            pl.BlockSpec((1, d, ts), lambda b, i: (b, 0, i)),
        ],
        out_shape=[
            jax.ShapeDtypeStruct((bsz, seq, qk_width), BF16),
            jax.ShapeDtypeStruct((bsz, d, seq), BF16),
        ],
        compiler_params=_cparams(("parallel", "parallel")),
        name="fox_prep",
    )(proj3, gains)


def _prefix_sum_lanes(x, period):
    pos = lax.broadcasted_iota(jnp.int32, x.shape, 1) % period
    d = 1
    while d < period:
        x = x + jnp.where(pos >= d, pltpu.roll(x, d, axis=1), 0.0)
        d *= 2
    return x


def _fox_gate_kernel(f_ref, b_ref, c_ref):
    z = f_ref[0] + b_ref[...]
    log_f = -_softplus(-z)
    c_ref[0] = _prefix_sum_lanes(log_f, z.shape[1])


def _fox_gates(f_logit_t, b_f):
    bsz, h, s = f_logit_t.shape
    return pl.pallas_call(
        _fox_gate_kernel,
        grid=(bsz,),
        in_specs=[pl.BlockSpec((1, h, s), lambda b: (b, 0, 0)),
                  pl.BlockSpec((h, 1), lambda b: (0, 0))],
        out_specs=pl.BlockSpec((1, h, s), lambda b: (b, 0, 0)),
        out_shape=jax.ShapeDtypeStruct((bsz, h, s), F32),
        compiler_params=_cparams(("parallel",)),
        name="fox_gates",
    )(f_logit_t, b_f.reshape(h, 1))


def _gdn_gate_kernel(b_ref, a_ref, alog_ref, dt_ref, beta_ref, gc_ref):
    beta_ref[0] = _sigmoid(b_ref[0])
    g = -jnp.exp(alog_ref[...]) * _softplus(a_ref[0] + dt_ref[...])
    gc_ref[0] = _prefix_sum_lanes(g, CHUNK)


def _gdn_gates(b_logit_t, a_logit_t, a_log, dt_bias):
    bsz, h, s = b_logit_t.shape
    spec = pl.BlockSpec((1, h, s), lambda b: (b, 0, 0))
    vec = pl.BlockSpec((h, 1), lambda b: (0, 0))
    return pl.pallas_call(
        _gdn_gate_kernel,
        grid=(bsz,),
        in_specs=[spec, spec, vec, vec],
        out_specs=[spec, spec],
        out_shape=[jax.ShapeDtypeStruct((bsz, h, s), F32)] * 2,
        compiler_params=_cparams(("parallel",)),
        name="gdn_gates",
    )(b_logit_t, a_logit_t, a_log.reshape(h, 1), dt_bias.reshape(h, 1))


def _fox_attn_kernel(q_ref, k_ref, vt_ref, crow_ref, o_ref, ckb_ref, acc_ref, sbuf_ref,
                     *, head_dim, tq, tk):
    qi = pl.program_id(2)
    seq = k_ref.shape[1]

    @pl.when(qi == 0)
    def _():
        def fill(j, carry):
            st = pl.multiple_of(j * LANES, LANES)
            rows = crow_ref[0, 0, :, pl.ds(st, LANES)] * LOG2E
            for hh in range(2):
                ckb_ref[hh, pl.ds(st, LANES), :] = jnp.broadcast_to(
                    rows[hh:hh + 1, :], (LANES, LANES)).T
            return carry
        lax.fori_loop(0, seq // LANES, fill, 0)

    lane = lax.broadcasted_iota(jnp.int32, (1, LANES), 1)
    first = lane < head_dim
    q = q_ref[0]
    zero = jnp.zeros_like(q)
    q_heads = (jnp.where(first, q, zero), jnp.where(first, zero, q))
    q_start = pl.multiple_of(qi * tq, tq)
    c_q = crow_ref[0, 0, :, pl.ds(q_start, tq)] * LOG2E
    acc_ref[...] = jnp.zeros(acc_ref.shape, F32)
    ones = jnp.ones((acc_ref.shape[1] - head_dim, tk), BF16)

    def scores(j):
        start = pl.multiple_of(j * tk, tk)
        k = k_ref[0, pl.ds(start, tk), :]
        return [lax.dot_general(k, q_heads[hh], (((1,), (1,)), ((), ())),
                                preferred_element_type=F32) for hh in range(2)]

    def update(j, s_in, stats, masked):
        start = pl.multiple_of(j * tk, tk)
        out = []
        for hh in range(2):
            m_old = stats[hh]
            s0 = s_in[hh]
            cb = ckb_ref[hh, pl.ds(start, tk), :]
            slabs = []
            for t in range(tq // LANES):
                sl = s0[:, t * LANES:(t + 1) * LANES] - cb
                if masked:
                    kpos = start + lax.broadcasted_iota(jnp.int32, sl.shape, 0)
                    qpos = q_start + t * LANES + lax.broadcasted_iota(jnp.int32, sl.shape, 1)
                    sl = jnp.where(kpos <= qpos, sl, -jnp.inf)
                slabs.append(sl)
            s = jnp.concatenate(slabs, axis=1)
            cq = c_q[hh:hh + 1, :]
            m_new = jnp.maximum(m_old, jnp.max(s, axis=0, keepdims=True) + cq)
            p = jnp.exp2(s + (cq - m_new))
            alpha = jnp.exp2(m_old - m_new)
            vt = jnp.concatenate(
                [vt_ref[0, hh * head_dim:(hh + 1) * head_dim, pl.ds(start, tk)], ones], axis=0)
            pv = jnp.dot(vt, p.astype(BF16), preferred_element_type=F32)
            acc_ref[hh] = acc_ref[hh] * alpha + pv
            out.append(m_new)
        return tuple(out)

    def park(s):
        sbuf_ref[0] = s[0]
        sbuf_ref[1] = s[1]

    def parked():
        return [sbuf_ref[0], sbuf_ref[1]]

    def pair(i, stats):
        j0 = 2 * i
        s1 = scores(j0 + 1)
        stats = update(j0, parked(), stats, False)
        park(scores(j0 + 2))
        return update(j0 + 1, s1, stats, False)

    neg = jnp.full((1, tq), -1e30, F32)
    park(scores(0))
    stats = lax.fori_loop(0, qi, pair, (neg, neg))
    j0 = 2 * qi
    s1 = scores(j0 + 1)
    stats = update(j0, parked(), stats, True)
    update(j0 + 1, s1, stats, True)
    o_t = jnp.concatenate(
        [acc_ref[hh, :head_dim, :] / acc_ref[hh, head_dim:head_dim + 1, :] for hh in range(2)],
        axis=0)
    o_ref[0] = o_t.T.astype(BF16)


def _fox_attention(qk3, v_t, c_row, n_heads, head_dim):
    bsz, seq, _ = qk3.shape
    d = n_heads * head_dim
    pairs = d // LANES
    tq = min(ATT_Q_TILE, seq)
    tk = tq // 2
    assert seq % tq == 0
    return pl.pallas_call(
        functools.partial(_fox_attn_kernel, head_dim=head_dim, tq=tq, tk=tk),
        grid=(bsz, pairs, seq // tq),
        in_specs=[
            pl.BlockSpec((1, tq, LANES), lambda b, p, i: (b, i, p)),
            pl.BlockSpec((1, seq, LANES), lambda b, p, i: (b, 0, pairs + p)),
            pl.BlockSpec((1, LANES, seq), lambda b, p, i: (b, p, 0)),
            pl.BlockSpec((1, 1, 2, seq), lambda b, p, i: (b, p, 0, 0)),
        ],
        out_specs=pl.BlockSpec((1, tq, LANES), lambda b, p, i: (b, i, p)),
        out_shape=jax.ShapeDtypeStruct((bsz, seq, d), BF16),
        scratch_shapes=[
            pltpu.VMEM((2, seq, LANES), F32),
            pltpu.VMEM((2, head_dim + BF16_ROWS, tq), F32),
            pltpu.VMEM((2, tk, tq), F32),
        ],
        compiler_params=_cparams(("parallel", "parallel", "arbitrary")),
        name="fox_attention",
    )(qk3, qk3, v_t, c_row)


def _gdn_prep_kernel(p_ref, w_ref, o_ref, buf_ref, *, d_model, head_dim, q_scale):
    ts = p_ref.shape[1]
    pad = 8

    @pl.when(pl.program_id(1) == 0)
    def _():
        buf_ref[0:pad, :] = jnp.zeros((pad, buf_ref.shape[1]), F32)

    buf_ref[pad:pad + ts, :] = p_ref[0].astype(F32)
    for j in range(o_ref.shape[2] // LANES):
        sl = slice(j * LANES, (j + 1) * LANES)
        acc = buf_ref[pad:pad + ts, sl] * w_ref[CONV_WIDTH - 1:CONV_WIDTH, sl]
        for dlt in range(1, CONV_WIDTH):
            acc = acc + (buf_ref[pad - dlt:pad - dlt + ts, sl]
                         * w_ref[CONV_WIDTH - 1 - dlt:CONV_WIDTH - dlt, sl])
        y = _silu(acc)
        if j * LANES < 2 * d_model:
            y = y * lax.rsqrt(jnp.sum(y * y, axis=-1, keepdims=True) + L2_EPS)
            if j * LANES < d_model:
                y = y * q_scale
        o_ref[0, :, sl] = y.astype(BF16)
    buf_ref[0:pad, :] = buf_ref[ts:ts + pad, :]


def _gdn_prep(proj3, conv_w, d_model, head_dim):
    bsz, seq, _ = proj3.shape
    width = 3 * d_model
    ts = min(ROW_TILE, seq)
    return pl.pallas_call(
        functools.partial(_gdn_prep_kernel, d_model=d_model, head_dim=head_dim,
                          q_scale=head_dim ** -0.5),
        grid=(bsz, seq // ts),
        in_specs=[
            pl.BlockSpec((1, ts, width), lambda b, i: (b, i, 0)),
            pl.BlockSpec((CONV_WIDTH, width), lambda b, i: (0, 0)),
        ],
        out_specs=pl.BlockSpec((1, ts, width), lambda b, i: (b, i, 0)),
        out_shape=jax.ShapeDtypeStruct((bsz, seq, width), BF16),
        scratch_shapes=[pltpu.VMEM((ts + 8, width), F32)],
        compiler_params=_cparams(("parallel", "arbitrary")),
        name="gdn_prep",
    )(proj3, conv_w)


def _gdn_chunk_kernel(q_ref, k_ref, v_ref, col_ref, row_ref, o_ref, state_ref,
                      *, n_heads, head_dim):
    @pl.when(pl.program_id(1) == 0)
    def _():
        state_ref[...] = jnp.zeros(state_ref.shape, F32)

    n_chunks = q_ref.shape[1] // CHUNK
    ri = lax.broadcasted_iota(jnp.int32, (CHUNK, CHUNK), 0)
    ci = lax.broadcasted_iota(jnp.int32, (CHUNK, CHUNK), 1)
    eye = jnp.where(ri == ci, 1.0, 0.0)
    heads = range(n_heads)
    nt = (((1,), (1,)), ((), ()))

    def mm(a, b):
        return jnp.dot(a, b, preferred_element_type=F32)

    def stack(a, b):
        return jnp.concatenate([a, b], axis=0)

    def chunk(it, carry):
        first = it * GDN_UNROLL
        starts = [pl.multiple_of((first + cc) * CHUNK, CHUNK) for cc in range(GDN_UNROLL)]
        cols = [col_ref[0, first + cc] for cc in range(GDN_UNROLL)]
        rows = [row_ref[0, first + cc] for cc in range(GDN_UNROLL)]
        probs = range(GDN_UNROLL * n_heads)
        cc_of = [p // n_heads for p in probs]
        h_of = [p % n_heads for p in probs]
        sls = [slice(h * head_dim, (h + 1) * head_dim) for h in heads]
        q = [q_ref[0, pl.ds(starts[cc_of[p]], CHUNK), sls[h_of[p]]] for p in probs]
        k = [k_ref[0, pl.ds(starts[cc_of[p]], CHUNK), sls[h_of[p]]] for p in probs]
        v = [v_ref[0, pl.ds(starts[cc_of[p]], CHUNK), sls[h_of[p]]] for p in probs]
        beta = [cols[cc_of[p]][:, h_of[p]:h_of[p] + 1] for p in probs]
        g_col = [cols[cc_of[p]][:, n_heads + h_of[p]:n_heads + h_of[p] + 1] for p in probs]
        g_row = [rows[cc_of[p]][n_heads + h_of[p]:n_heads + h_of[p] + 1, :] for p in probs]
        g_last = [g_col[p][CHUNK - 1:CHUNK, :] for p in probs]
        decay = [jnp.exp(jnp.where(ri >= ci, g_col[p] - g_row[p], -jnp.inf)) for p in probs]
        kf = [k[p].astype(F32) for p in probs]
        kb = [kf[p] * beta[p] for p in probs]

        r1 = [lax.dot_general(stack(kb[p].astype(BF16), q[p]), k[p], nt,
                              preferred_element_type=F32) for p in probs]
        m = [jnp.where(ri > ci, r1[p][:CHUNK] * decay[p], 0.0) for p in probs]
        attn = [(r1[p][CHUNK:] * decay[p]).astype(BF16) for p in probs]

        t = [eye - m[p] for p in probs]
        pb = [m[p].astype(BF16) for p in probs]
        pw = [mm(pb[p], pb[p]) for p in probs]
        power = 2
        while power < CHUNK:
            pb = [pw[p].astype(BF16) for p in probs]
            if 2 * power >= CHUNK:
                t = [t[p] + mm(t[p].astype(BF16), pb[p]) for p in probs]
            else:
                r = [mm(stack(t[p].astype(BF16), pb[p]), pb[p]) for p in probs]
                t = [t[p] + r[p][:CHUNK] for p in probs]
                pw = [r[p][CHUNK:] for p in probs]
            power *= 2

        e_g = [jnp.exp(g_col[p]) for p in probs]
        rhs = [jnp.concatenate([v[p].astype(F32) * beta[p], kb[p] * e_g[p]], axis=1).astype(BF16)
               for p in probs]
        sol = [mm(t[p].astype(BF16), rhs[p]) for p in probs]
        q_dec = [(q[p].astype(F32) * e_g[p]).astype(BF16) for p in probs]
        k_dec_t = [(kf[p] * jnp.exp(g_last[p] - g_col[p])).T.astype(BF16) for p in probs]

        for cc in range(GDN_UNROLL):
            ps = [cc * n_heads + h for h in heads]
            sb = [state_ref[h].astype(BF16) for h in heads]
            r2 = [mm(stack(sol[ps[h]][:, head_dim:].astype(BF16), q_dec[ps[h]]), sb[h])
                  for h in heads]
            vb = [(sol[ps[h]][:, :head_dim] - r2[h][:CHUNK]).astype(BF16) for h in heads]
            r3 = [mm(stack(attn[ps[h]], k_dec_t[ps[h]]), vb[h]) for h in heads]
            for h in heads:
                o_ref[0, pl.ds(starts[cc], CHUNK), sls[h]] = (
                    r2[h][CHUNK:] + r3[h][:CHUNK]).astype(BF16)
                state_ref[h] = state_ref[h] * jnp.exp(g_last[ps[h]]) + r3[h][CHUNK:]
        return carry

    lax.fori_loop(0, n_chunks // GDN_UNROLL, chunk, 0)


def _gdn_delta_rule(qkv3, cols, rows, n_heads, head_dim):
    bsz, seq, _ = qkv3.shape
    d = n_heads * head_dim
    ts = min(GDN_TILE, seq)
    cpt = ts // CHUNK
    return pl.pallas_call(
        functools.partial(_gdn_chunk_kernel, n_heads=n_heads, head_dim=head_dim),
        grid=(bsz, seq // ts),
        in_specs=[
            pl.BlockSpec((1, ts, d), lambda b, i: (b, i, 0)),
            pl.BlockSpec((1, ts, d), lambda b, i: (b, i, 1)),
            pl.BlockSpec((1, ts, d), lambda b, i: (b, i, 2)),
            pl.BlockSpec((1, cpt, CHUNK, 2 * n_heads), lambda b, i: (b, i, 0, 0)),
            pl.BlockSpec((1, cpt, 2 * n_heads, CHUNK), lambda b, i: (b, i, 0, 0)),
        ],
        out_specs=pl.BlockSpec((1, ts, d), lambda b, i: (b, i, 0)),
        out_shape=jax.ShapeDtypeStruct((bsz, seq, d), BF16),
        scratch_shapes=[pltpu.VMEM((n_heads, head_dim, head_dim), F32)],
        compiler_params=_cparams(("parallel", "arbitrary")),
        name="gdn_delta_rule",
    )(qkv3, qkv3, qkv3, cols, rows)


def _route_rows(logits_t, bias):
    n_exp = logits_t.shape[0]
    mx = jnp.max(logits_t, axis=0, keepdims=True)
    ex = jnp.exp(logits_t - mx)
    probs = ex / jnp.sum(ex, axis=0, keepdims=True)
    sel = probs + bias
    sel_r = [sel[e:e + 1, :] for e in range(n_exp)]
    prob_r = [probs[e:e + 1, :] for e in range(n_exp)]

    def top2_sum(a, b, c, d):
        hi1, lo1 = jnp.maximum(a, b), jnp.minimum(a, b)
        hi2, lo2 = jnp.maximum(c, d), jnp.minimum(c, d)
        return jnp.maximum(hi1, hi2) + jnp.maximum(jnp.minimum(hi1, hi2), jnp.maximum(lo1, lo2))

    scores = [top2_sum(*sel_r[EXPERTS_PER_GROUP * g:EXPERTS_PER_GROUP * (g + 1)])
              for g in range(N_GROUPS)]
    best = scores[0]
    gidx = jnp.zeros(best.shape, jnp.int32)
    for g in range(1, N_GROUPS):
        upd = scores[g] > best
        best = jnp.where(upd, scores[g], best)
        gidx = jnp.where(upd, g, gidx)

    def pick(vals, j):
        out = vals[j]
        for g in range(1, N_GROUPS):
            out = jnp.where(gidx == g, vals[EXPERTS_PER_GROUP * g + j], out)
        return out

    cand = [pick(sel_r, j) for j in range(EXPERTS_PER_GROUP)]
    cprob = [pick(prob_r, j) for j in range(EXPERTS_PER_GROUP)]
    b1, i1, p1 = cand[0], jnp.zeros(best.shape, jnp.int32), cprob[0]
    for j in range(1, EXPERTS_PER_GROUP):
        upd = cand[j] > b1
        b1 = jnp.where(upd, cand[j], b1)
        i1 = jnp.where(upd, j, i1)
        p1 = jnp.where(upd, cprob[j], p1)
    b2 = jnp.full(best.shape, -jnp.inf, F32)
    i2 = jnp.full(best.shape, -1, jnp.int32)
    p2 = jnp.zeros(best.shape, F32)
    for j in range(EXPERTS_PER_GROUP):
        upd = jnp.logical_and(i1 != j, jnp.logical_or(cand[j] > b2, i2 < 0))
        b2 = jnp.where(upd, cand[j], b2)
        i2 = jnp.where(upd, j, i2)
        p2 = jnp.where(upd, cprob[j], p2)
    tot = p1 + p2
    base = gidx * EXPERTS_PER_GROUP
    return base + i1, base + i2, p1 / tot, p2 / tot


def _mix_out_kernel(o_ref, og_ref, h_ref, w_ref, lng_ref, lnb_ref, rw_ref, rb_ref,
                    *rest, alpha, gdn_head_dim):
    if gdn_head_dim:
        ng_ref, hn_ref, route_ref = rest
    else:
        hn_ref, route_ref = rest
    o = o_ref[...].astype(F32)
    og = og_ref[...].astype(F32)
    if gdn_head_dim:
        parts = []
        for j in range(o.shape[1] // gdn_head_dim):
            x = o[:, j * gdn_head_dim:(j + 1) * gdn_head_dim]
            r = lax.rsqrt(jnp.mean(x * x, axis=-1, keepdims=True) + RMS_EPS)
            parts.append(x * r * ng_ref[...])
        a = jnp.concatenate(parts, axis=1) * _silu(og)
    else:
        a = o * _sigmoid(og)
    mix = jnp.dot(a.astype(BF16), w_ref[...], preferred_element_type=F32)
    hn = _layer_norm_rows(alpha * h_ref[...] + mix, lng_ref[...], lnb_ref[...])
    hn_ref[...] = hn
    logits_t = lax.dot_general(rw_ref[...], hn, (((1,), (1,)), ((), ())),
                               preferred_element_type=F32,
                               precision=lax.Precision.HIGHEST)
    e1, e2, g1, g2 = _route_rows(logits_t, rb_ref[...])
    route_ref[0:1, :] = e1.astype(F32)
    route_ref[1:2, :] = e2.astype(F32)
    route_ref[2:3, :] = g1
    route_ref[3:4, :] = g2
    route_ref[4:8, :] = jnp.zeros((4, g1.shape[1]), F32)


def _mix_out(o, proj, og_col_block, h, w_out, ln_g, ln_b, rw_t, rb, alpha,
             norm_g=None, gdn_head_dim=0):
    n, d = o.shape
    n_exp = rw_t.shape[0]
    tm = min(ROW_TILE, n)
    row = lambda i: (i, 0)
    const = lambda i: (0, 0)
    in_specs = [
        pl.BlockSpec((tm, d), row),
        pl.BlockSpec((tm, d), lambda i: (i, og_col_block)),
        pl.BlockSpec((tm, d), row),
        pl.BlockSpec((d, d), const),
        pl.BlockSpec((1, d), const),
        pl.BlockSpec((1, d), const),
        pl.BlockSpec((n_exp, d), const),
        pl.BlockSpec((n_exp, 1), const),
    ]
    args = [o, proj, h, w_out, ln_g.reshape(1, d), ln_b.reshape(1, d), rw_t,
            rb.reshape(n_exp, 1)]
    if gdn_head_dim:
        in_specs.append(pl.BlockSpec((1, gdn_head_dim), const))
        args.append(norm_g.reshape(1, gdn_head_dim))
    return pl.pallas_call(
        functools.partial(_mix_out_kernel, alpha=alpha, gdn_head_dim=gdn_head_dim),
        grid=(n // tm,),
        in_specs=in_specs,
        out_specs=[pl.BlockSpec((tm, d), row), pl.BlockSpec((8, tm), lambda i: (0, i))],
        out_shape=[jax.ShapeDtypeStruct((n, d), F32), jax.ShapeDtypeStruct((8, n), F32)],
        compiler_params=_cparams(("parallel",)),
        name="mix_out",
    )(*args)


def _moe_kernel(elo_ref, ehi_ref, nblk_ref, src_ref, src_next_ref, dst_ref, gate_ref, h_hbm,
                wg_lo_ref, wu_lo_ref, wd_lo_ref, wg_hi_ref, wu_hi_ref, wd_hi_ref,
                lng_ref, lnb_ref, out_hbm,
                xbuf, ybuf, wgu_c, wd_c, gsem, ssem, *, alpha, d_expert, n_tok):
    i = pl.program_id(0)
    nblk = nblk_ref[0]
    slot = lax.rem(i, 2)
    blk = xbuf.shape[1]
    prev = jnp.maximum(i - 1, 0)

    def gather(idx_ref, dst):
        for r in range(blk):
            pltpu.make_async_copy(h_hbm.at[pl.ds(idx_ref[0, 0, r], 1)],
                                  xbuf.at[dst, pl.ds(r, 1)], gsem.at[dst]).start()

    def wait_gather(dst):
        pltpu.make_async_copy(h_hbm.at[pl.ds(0, blk)], xbuf.at[dst], gsem.at[dst]).wait()

    def wait_scatter(src):
        pltpu.make_async_copy(ybuf.at[src], out_hbm.at[pl.ds(0, blk)], ssem.at[src]).wait()

    @pl.when(i == 0)
    def _():
        ybuf[...] = jnp.zeros(ybuf.shape, F32)
        spare = [pltpu.make_async_copy(ybuf.at[s], out_hbm.at[pl.ds(n_tok + s * blk, blk)],
                                       ssem.at[s]) for s in range(2)]
        for cp in spare:
            cp.start()
        for cp in spare:
            cp.wait()
        gather(src_ref, 0)

    @pl.when(i + 1 < nblk)
    def _():
        gather(src_next_ref, 1 - slot)

    def refresh(which, e_ref, wg_ref, wu_ref, wd_ref):
        @pl.when(jnp.logical_or(i == 0, e_ref[i] != e_ref[prev]))
        def _():
            wgu_c[which, :, :d_expert] = wg_ref[0, 0].astype(BF16)
            wgu_c[which, :, d_expert:] = wu_ref[0, 0].astype(BF16)
            wd_c[which] = wd_ref[0, 0].astype(BF16)

    refresh(0, elo_ref, wg_lo_ref, wu_lo_ref, wd_lo_ref)
    refresh(1, ehi_ref, wg_hi_ref, wu_hi_ref, wd_hi_ref)

    @pl.when(i < nblk)
    def _():
        wait_gather(slot)
        x = xbuf[slot]
        xb = x.astype(BF16)

        def ffn(which):
            a = jnp.dot(xb, wgu_c[which], preferred_element_type=F32)
            hid = _silu(a[:, :d_expert]) * a[:, d_expert:]
            return jnp.dot(hid.astype(BF16), wd_c[which], preferred_element_type=F32)

        y = ffn(0) * gate_ref[:, 0:1] + ffn(1) * gate_ref[:, 1:2]
        res = _layer_norm_rows(alpha * x + y, lng_ref[...], lnb_ref[...])

        @pl.when(i >= 2)
        def _():
            wait_scatter(slot)

        ybuf[slot] = res
        for r in range(blk):
            pltpu.make_async_copy(ybuf.at[slot, pl.ds(r, 1)],
                                  out_hbm.at[pl.ds(dst_ref[0, 0, r], 1)], ssem.at[slot]).start()

        @pl.when(i == nblk - 1)
        def _():
            wait_scatter(slot)

            @pl.when(i >= 1)
            def _():
                wait_scatter(1 - slot)


def _moe_ffn(h, n_tok, src_idx, dst_idx, gates_sorted, blk_lo, blk_hi, n_used, wg, wu, wd,
             layer, ln_g, ln_b, alpha):
    d = h.shape[1]
    d_expert = wd.shape[2]
    lo_blk = lambda i, lo, hi, nb: (layer, lo[i], 0, 0)
    hi_blk = lambda i, lo, hi, nb: (layer, hi[i], 0, 0)
    n_blocks = src_idx.shape[0] // MOE_BLOCK
    src = src_idx.reshape(n_blocks, 1, MOE_BLOCK)
    dst = dst_idx.reshape(n_blocks, 1, MOE_BLOCK)
    last = n_blocks - 1
    smem_blk = functools.partial(pl.BlockSpec, (1, 1, MOE_BLOCK), memory_space=pltpu.SMEM)
    grid_spec = pltpu.PrefetchScalarGridSpec(
        num_scalar_prefetch=3,
        grid=(n_blocks,),
        in_specs=[
            smem_blk(lambda i, lo, hi, nb: (i, 0, 0)),
            smem_blk(lambda i, lo, hi, nb: (jnp.minimum(i + 1, last), 0, 0)),
            smem_blk(lambda i, lo, hi, nb: (i, 0, 0)),
            pl.BlockSpec((MOE_BLOCK, 2), lambda i, lo, hi, nb: (i, 0)),
            pl.BlockSpec(memory_space=pl.ANY),
            pl.BlockSpec((1, 1, d, d_expert), lo_blk),
            pl.BlockSpec((1, 1, d, d_expert), lo_blk),
            pl.BlockSpec((1, 1, d_expert, d), lo_blk),
            pl.BlockSpec((1, 1, d, d_expert), hi_blk),
            pl.BlockSpec((1, 1, d, d_expert), hi_blk),
            pl.BlockSpec((1, 1, d_expert, d), hi_blk),
            pl.BlockSpec((1, d), lambda i, lo, hi, nb: (0, 0)),
            pl.BlockSpec((1, d), lambda i, lo, hi, nb: (0, 0)),
        ],
        out_specs=pl.BlockSpec(memory_space=pl.ANY),
        scratch_shapes=[
            pltpu.VMEM((2, MOE_BLOCK, d), F32),
            pltpu.VMEM((2, MOE_BLOCK, d), F32),
            pltpu.VMEM((2, d, 2 * d_expert), BF16),
            pltpu.VMEM((2, d_expert, d), BF16),
            pltpu.SemaphoreType.DMA((2,)),
            pltpu.SemaphoreType.DMA((2,)),
        ],
    )
    return pl.pallas_call(
        functools.partial(_moe_kernel, alpha=alpha, d_expert=d_expert, n_tok=n_tok),
        grid_spec=grid_spec,
        out_shape=jax.ShapeDtypeStruct((n_tok + MOE_SPARE_ROWS, d), F32),
        compiler_params=_cparams(("arbitrary",)),
        name="moe_ffn",
    )(blk_lo, blk_hi, n_used, src, src, dst, gates_sorted, h, wg, wu, wd, wg, wu, wd,
      ln_g.reshape(1, d), ln_b.reshape(1, d))


def _dispatch_plan(route, n_tok):
    e1 = route[0].astype(jnp.int32)
    e2 = route[1].astype(jnp.int32)
    first_is_lo = e1 < e2
    lo = jnp.where(first_is_lo, e1, e2)
    hi = jnp.where(first_is_lo, e2, e1)
    g_lo = jnp.where(first_is_lo, route[2], route[3])
    g_hi = jnp.where(first_is_lo, route[3], route[2])
    grp = lo // EXPERTS_PER_GROUP
    a = lo % EXPERTS_PER_GROUP
    b = hi % EXPERTS_PER_GROUP
    cls = grp * PAIRS_PER_GROUP + (a * (7 - a)) // 2 + (b - a - 1)
    n_cls = N_GROUPS * PAIRS_PER_GROUP
    n_blocks = -(-(n_tok + n_cls * (MOE_BLOCK - 1)) // MOE_BLOCK)
    n_slots = n_blocks * MOE_BLOCK

    counts = jnp.bincount(cls, length=n_cls)
    start = jnp.cumsum(counts) - counts
    padded = (counts + MOE_BLOCK - 1) // MOE_BLOCK * MOE_BLOCK
    pend = jnp.cumsum(padded)
    pstart = pend - padded
    order = jnp.argsort(cls, stable=True).astype(jnp.int32)
    slot_ids = jnp.arange(n_slots)
    slot_cls = jnp.minimum(jnp.sum(slot_ids[:, None] >= pend[None, :], axis=1), n_cls - 1)
    rank = slot_ids - pstart[slot_cls]
    valid = rank < counts[slot_cls]
    slot_tok = jnp.where(valid, order[jnp.clip(start[slot_cls] + rank, 0, n_tok - 1)], 0)
    slot_tok = slot_tok.astype(jnp.int32)
    spare = n_tok + (slot_ids // MOE_BLOCK % 2) * MOE_BLOCK + slot_ids % MOE_BLOCK
    slot_dst = jnp.where(valid, slot_tok, spare).astype(jnp.int32)
    blk_cls = slot_cls[::MOE_BLOCK]
    pair_lo = jnp.array([0, 0, 0, 1, 1, 2], jnp.int32)
    pair_hi = jnp.array([1, 2, 3, 2, 3, 3], jnp.int32)
    blk_grp = blk_cls // PAIRS_PER_GROUP
    blk_pair = blk_cls % PAIRS_PER_GROUP
    blk_lo = (blk_grp * EXPERTS_PER_GROUP + pair_lo[blk_pair]).astype(jnp.int32)
    blk_hi = (blk_grp * EXPERTS_PER_GROUP + pair_hi[blk_pair]).astype(jnp.int32)
    n_used = (pend[-1] // MOE_BLOCK).astype(jnp.int32).reshape(1)
    gates_sorted = jnp.stack([g_lo[slot_tok], g_hi[slot_tok]], axis=1)
    return slot_tok, slot_dst, gates_sorted, blk_lo, blk_hi, n_used


def _moe_layer(h, route, wg, wu, wd, layer, ln_g, ln_b, alpha):
    n_tok = h.shape[0]
    slot_tok, slot_dst, gates_sorted, blk_lo, blk_hi, n_used = _dispatch_plan(route, n_tok)
    return _moe_ffn(h, n_tok, slot_tok, slot_dst, gates_sorted, blk_lo, blk_hi, n_used,
                    wg, wu, wd, layer, ln_g, ln_b, alpha)


def _pad_cols(w, width):
    return jnp.pad(w, ((0, 0), (0, width - w.shape[1])))


def kernel(x, ln_mix_g, ln_mix_b, ln_ffn_g, ln_ffn_b, router_w, router_b, fox_w_in, fox_b_f,
           fox_q_gain, fox_k_gain, fox_w_out, gdn_w_in, gdn_conv_w, gdn_a_log, gdn_dt_bias,
           gdn_norm_g, gdn_w_out, moe_w_gate, moe_w_up, moe_w_down):
    bsz, seq, d = x.shape
    n = bsz * seq
    depth = ln_mix_g.shape[0]
    alpha = (2.0 * depth) ** 0.25
    fox_heads = fox_b_f.shape[1]
    fox_dim = d // fox_heads
    gdn_heads = gdn_a_log.shape[1]
    gdn_dim = d // gdn_heads
    rw_t = router_w.T

    h = x.reshape(n, d)
    for i in range(depth):
        j = i // 2
        if i % 2 == 0:
            w_in = fox_w_in[j]
            proj, side = _inproj(h, n, w_in[:, :4 * d].astype(BF16),
                                 _pad_cols(w_in[:, 4 * d:], LANES).astype(BF16))
            gains = jnp.concatenate([jnp.tile(fox_q_gain[j], fox_heads) * (fox_dim ** -0.5 * LOG2E),
                                     jnp.tile(fox_k_gain[j], fox_heads)]).reshape(1, 2 * d)
            qk3, v_t = _fox_prep(proj.reshape(bsz, seq, 4 * d), gains, fox_dim)
            f_t = side[:, :fox_heads].reshape(bsz, seq, fox_heads).transpose(0, 2, 1)
            c = _fox_gates(f_t, fox_b_f[j])
            c_row = c.reshape(bsz, fox_heads // 2, 2, seq)
            o = _fox_attention(qk3, v_t, c_row, fox_heads, fox_dim)
            h, route = _mix_out(o.reshape(n, d), proj, 3, h, fox_w_out[j].astype(BF16),
                                ln_mix_g[i], ln_mix_b[i], rw_t, router_b, alpha)
        else:
            w_in = gdn_w_in[j]
            proj, side = _inproj(h, n, w_in[:, :4 * d].astype(BF16),
                                 _pad_cols(w_in[:, 4 * d:], LANES).astype(BF16))
            qkv = _gdn_prep(proj.reshape(bsz, seq, 4 * d), gdn_conv_w[j], d, gdn_dim)
            side3 = side.reshape(bsz, seq, LANES)
            b_t = side3[:, :, :gdn_heads].transpose(0, 2, 1)
            a_t = side3[:, :, gdn_heads:2 * gdn_heads].transpose(0, 2, 1)
            beta, gc = _gdn_gates(b_t, a_t, gdn_a_log[j], gdn_dt_bias[j])
            rows = jnp.concatenate([beta, gc], axis=1).reshape(
                bsz, 2 * gdn_heads, seq // CHUNK, CHUNK).transpose(0, 2, 1, 3)
            cols = rows.transpose(0, 1, 3, 2)
            o = _gdn_delta_rule(qkv, cols, rows, gdn_heads, gdn_dim)
            h, route = _mix_out(o.reshape(n, d), proj, 3, h, gdn_w_out[j].astype(BF16),
                                ln_mix_g[i], ln_mix_b[i], rw_t, router_b, alpha,
                                norm_g=gdn_norm_g[j], gdn_head_dim=gdn_dim)
        h = _moe_layer(h, route, moe_w_gate, moe_w_up, moe_w_down, i,
                       ln_ffn_g[i], ln_ffn_b[i], alpha)
    return h[:n].reshape(bsz, seq, d)
```

```python
import functools
import math

import jax
import jax.numpy as jnp
from jax import lax
from jax.experimental import pallas as pl
from jax.experimental.pallas import tpu as pltpu

F32 = jnp.float32
BF16 = jnp.bfloat16

N_GROUPS = 4
EXPERTS_PER_GROUP = 4
PAIRS_PER_GROUP = 6
CHUNK = 64
CONV_WIDTH = 4
LN_EPS = 1e-5
RMS_EPS = 1e-6
L2_EPS = 1e-6

LOG2E = math.log2(math.e)

LANES = 128
BF16_ROWS = 16
VMEM_LIMIT = 56 * 1024 * 1024

ROW_TILE = 512
ATT_Q_TILE = 512
GDN_TILE = 256
GDN_UNROLL = 2
MOE_BLOCK = 256
MOE_SPARE_ROWS = 2 * MOE_BLOCK


def _cparams(sem):
    return pltpu.CompilerParams(dimension_semantics=sem, vmem_limit_bytes=VMEM_LIMIT)


def _layer_norm_rows(y, g, b):
    mu = jnp.mean(y, axis=-1, keepdims=True)
    yc = y - mu
    var = jnp.mean(yc * yc, axis=-1, keepdims=True)
    return yc * lax.rsqrt(var + LN_EPS) * g + b


def _sigmoid(x):
    return 1.0 / (1.0 + jnp.exp(-x))


def _silu(x):
    return x * _sigmoid(x)


def _softplus(x):
    return jnp.maximum(x, 0.0) + jnp.log(1.0 + jnp.exp(-jnp.abs(x)))


def _inproj_kernel(x_ref, w_ref, ws_ref, o_ref, os_ref, *, col_chunk):
    xb = x_ref[...].astype(BF16)
    width = o_ref.shape[1]
    for j in range(width // col_chunk):
        sl = slice(j * col_chunk, (j + 1) * col_chunk)
        o_ref[:, sl] = jnp.dot(xb, w_ref[:, sl], preferred_element_type=F32).astype(BF16)
    os_ref[...] = jnp.dot(xb, ws_ref[...], preferred_element_type=F32)


def _inproj(x, n, w_main, w_side):
    d = x.shape[1]
    width = w_main.shape[1]
    tm = min(ROW_TILE, n)
    return pl.pallas_call(
        functools.partial(_inproj_kernel, col_chunk=1024),
        grid=(n // tm,),
        in_specs=[
            pl.BlockSpec((tm, d), lambda i: (i, 0)),
            pl.BlockSpec((d, width), lambda i: (0, 0)),
            pl.BlockSpec((d, LANES), lambda i: (0, 0)),
        ],
        out_specs=[
            pl.BlockSpec((tm, width), lambda i: (i, 0)),
            pl.BlockSpec((tm, LANES), lambda i: (i, 0)),
        ],
        out_shape=[
            jax.ShapeDtypeStruct((n, width), BF16),
            jax.ShapeDtypeStruct((n, LANES), F32),
        ],
        compiler_params=_cparams(("parallel",)),
        name="inproj",
    )(x, w_main, w_side)


def _fox_prep_kernel(p_ref, gain_ref, c_ref, place_ref, q_ref, kaug_ref, vt_ref, *, head_dim):
    lane = lax.broadcasted_iota(jnp.int32, (1, LANES), 1)
    first = lane < head_dim
    d = q_ref.shape[2]

    def normed(j):
        sl = slice(j * LANES, (j + 1) * LANES)
        x = p_ref[0, :, sl].astype(F32)
        xx = x * x
        s_a = jnp.sum(jnp.where(first, xx, 0.0), axis=-1, keepdims=True)
        s_b = jnp.sum(jnp.where(first, 0.0, xx), axis=-1, keepdims=True)
        r = jnp.where(first, lax.rsqrt(s_a / head_dim + RMS_EPS),
                      lax.rsqrt(s_b / head_dim + RMS_EPS))
        return (x * r * gain_ref[:, sl]).astype(BF16)

    c = c_ref[0] * LOG2E
    aug = None
    for t in range(3):
        piece = c.astype(BF16)
        c = c - piece.astype(F32)
        term = jnp.dot(piece, place_ref[t], preferred_element_type=F32)
        aug = term if aug is None else aug + term
    pairs = d // LANES
    for j in range(pairs):
        q_ref[0, :, j * LANES:(j + 1) * LANES] = normed(j)
        kaug_ref[0, :, 2 * j * LANES:(2 * j + 1) * LANES] = normed(pairs + j)
        kaug_ref[0, :, (2 * j + 1) * LANES:(2 * j + 2) * LANES] = (
            aug[:, j * LANES:(j + 1) * LANES].astype(BF16))
    for j in range(pairs):
        v = p_ref[0, :, 2 * d + j * LANES:2 * d + (j + 1) * LANES].astype(F32)
        vt_ref[0, j * LANES:(j + 1) * LANES, :] = v.T.astype(BF16)


def _gate_placement(n_heads, d):
    place = jnp.zeros((3, n_heads, d), F32)
    h = jnp.arange(n_heads)
    for t in range(3):
        place = place.at[t, h, (h // 2) * LANES + 3 * (h % 2) + t].set(1.0)
    return place.astype(BF16)


def _fox_prep(proj3, gains, c_cols, head_dim):
    bsz, seq, _ = proj3.shape
    d = gains.shape[1] // 2
    n_heads = c_cols.shape[2]
    ts = min(ROW_TILE, seq)
    return pl.pallas_call(
        functools.partial(_fox_prep_kernel, head_dim=head_dim),
        grid=(bsz, seq // ts),
        in_specs=[
            pl.BlockSpec((1, ts, 3 * d), lambda b, i: (b, i, 0)),
            pl.BlockSpec((1, 2 * d), lambda b, i: (0, 0)),
            pl.BlockSpec((1, ts, n_heads), lambda b, i: (b, i, 0)),
            pl.BlockSpec((3, n_heads, d), lambda b, i: (0, 0, 0)),
        ],
        out_specs=[
            pl.BlockSpec((1, ts, d), lambda b, i: (b, i, 0)),
            pl.BlockSpec((1, ts, 2 * d), lambda b, i: (b, i, 0)),
            pl.BlockSpec((1, d, ts), lambda b, i: (b, 0, i)),
        ],
        out_shape=[
            jax.ShapeDtypeStruct((bsz, seq, d), BF16),
            jax.ShapeDtypeStruct((bsz, seq, 2 * d), BF16),
            jax.ShapeDtypeStruct((bsz, d, seq), BF16),
        ],
        compiler_params=_cparams(("parallel", "parallel")),
        name="fox_prep",
    )(proj3, gains, c_cols, _gate_placement(n_heads, d))


def _prefix_sum_lanes(x, period):
    pos = lax.broadcasted_iota(jnp.int32, x.shape, 1) % period
    d = 1
    while d < period:
        x = x + jnp.where(pos >= d, pltpu.roll(x, d, axis=1), 0.0)
        d *= 2
    return x


def _fox_gate_kernel(f_ref, b_ref, c_ref):
    z = f_ref[0] + b_ref[...]
    log_f = -_softplus(-z)
    c_ref[0] = _prefix_sum_lanes(log_f, z.shape[1])


def _fox_gates(f_logit_t, b_f):
    bsz, h, s = f_logit_t.shape
    return pl.pallas_call(
        _fox_gate_kernel,
        grid=(bsz,),
        in_specs=[pl.BlockSpec((1, h, s), lambda b: (b, 0, 0)),
                  pl.BlockSpec((h, 1), lambda b: (0, 0))],
        out_specs=pl.BlockSpec((1, h, s), lambda b: (b, 0, 0)),
        out_shape=jax.ShapeDtypeStruct((bsz, h, s), F32),
        compiler_params=_cparams(("parallel",)),
        name="fox_gates",
    )(f_logit_t, b_f.reshape(h, 1))


def _gdn_gate_kernel(b_ref, a_ref, alog_ref, dt_ref, beta_ref, gc_ref):
    beta_ref[0] = _sigmoid(b_ref[0])
    g = -jnp.exp(alog_ref[...]) * _softplus(a_ref[0] + dt_ref[...])
    gc_ref[0] = _prefix_sum_lanes(g, CHUNK)


def _gdn_gates(b_logit_t, a_logit_t, a_log, dt_bias):
    bsz, h, s = b_logit_t.shape
    spec = pl.BlockSpec((1, h, s), lambda b: (b, 0, 0))
    vec = pl.BlockSpec((h, 1), lambda b: (0, 0))
    return pl.pallas_call(
        _gdn_gate_kernel,
        grid=(bsz,),
        in_specs=[spec, spec, vec, vec],
        out_specs=[spec, spec],
        out_shape=[jax.ShapeDtypeStruct((bsz, h, s), F32)] * 2,
        compiler_params=_cparams(("parallel",)),
        name="gdn_gates",
    )(b_logit_t, a_logit_t, a_log.reshape(h, 1), dt_bias.reshape(h, 1))


def _fox_attn_kernel(q_ref, k_ref, vt_ref, crow_ref, o_ref, acc_ref, sbuf_ref,
                     *, head_dim, tq, tk):
    qi = pl.program_id(2)
    lane = lax.broadcasted_iota(jnp.int32, (1, LANES), 1)
    first = lane < head_dim
    q = q_ref[0]
    zero = jnp.zeros_like(q)
    minus = [jnp.broadcast_to(jnp.where((lane >= 3 * hh) & (lane < 3 * hh + 3), -1.0, 0.0),
                              (tq, LANES)).astype(BF16) for hh in range(2)]
    q_heads = (jnp.concatenate([jnp.where(first, q, zero), minus[0]], axis=1),
               jnp.concatenate([jnp.where(first, zero, q), minus[1]], axis=1))
    q_start = pl.multiple_of(qi * tq, tq)
    c_q = crow_ref[0, 0, :, pl.ds(q_start, tq)] * LOG2E
    acc_ref[...] = jnp.zeros(acc_ref.shape, F32)
    ones = jnp.ones((acc_ref.shape[1] - head_dim, tk), BF16)

    def scores(j):
        start = pl.multiple_of(j * tk, tk)
        k = k_ref[0, pl.ds(start, tk), :]
        return [lax.dot_general(k, q_heads[hh], (((1,), (1,)), ((), ())),
                                preferred_element_type=F32) for hh in range(2)]

    def update(j, s_in, stats, masked):
        start = pl.multiple_of(j * tk, tk)
        out = []
        for hh in range(2):
            m_old = stats[hh]
            s = s_in[hh]
            if masked:
                kpos = start + lax.broadcasted_iota(jnp.int32, s.shape, 0)
                qpos = q_start + lax.broadcasted_iota(jnp.int32, s.shape, 1)
                s = jnp.where(kpos <= qpos, s, -jnp.inf)
            cq = c_q[hh:hh + 1, :]
            m_new = jnp.maximum(m_old, jnp.max(s, axis=0, keepdims=True) + cq)
            p = jnp.exp2(s + (cq - m_new))
            alpha = jnp.exp2(m_old - m_new)
            vt = jnp.concatenate(
                [vt_ref[0, hh * head_dim:(hh + 1) * head_dim, pl.ds(start, tk)], ones], axis=0)
            pv = jnp.dot(vt, p.astype(BF16), preferred_element_type=F32)
            acc_ref[hh] = acc_ref[hh] * alpha + pv
            out.append(m_new)
        return tuple(out)

    def park(s):
        sbuf_ref[0] = s[0]
        sbuf_ref[1] = s[1]

    def parked():
        return [sbuf_ref[0], sbuf_ref[1]]

    def pair(i, stats):
        j0 = 2 * i
        s1 = scores(j0 + 1)
        stats = update(j0, parked(), stats, False)
        park(scores(j0 + 2))
        return update(j0 + 1, s1, stats, False)

    neg = jnp.full((1, tq), -1e30, F32)
    park(scores(0))
    stats = lax.fori_loop(0, qi, pair, (neg, neg))
    j0 = 2 * qi
    s1 = scores(j0 + 1)
    stats = update(j0, parked(), stats, True)
    update(j0 + 1, s1, stats, True)
    o_t = jnp.concatenate(
        [acc_ref[hh, :head_dim, :] / acc_ref[hh, head_dim:head_dim + 1, :] for hh in range(2)],
        axis=0)
    o_ref[0] = o_t.T.astype(BF16)


def _fox_attention(q3, kaug3, v_t, c_row, n_heads, head_dim):
    bsz, seq, d = q3.shape
    pairs = d // LANES
    tq = min(ATT_Q_TILE, seq)
    tk = tq // 2
    assert seq % tq == 0
    return pl.pallas_call(
        functools.partial(_fox_attn_kernel, head_dim=head_dim, tq=tq, tk=tk),
        grid=(bsz, pairs, seq // tq),
        in_specs=[
            pl.BlockSpec((1, tq, LANES), lambda b, p, i: (b, i, p)),
            pl.BlockSpec((1, seq, 2 * LANES), lambda b, p, i: (b, 0, p)),
            pl.BlockSpec((1, LANES, seq), lambda b, p, i: (b, p, 0)),
            pl.BlockSpec((1, 1, 2, seq), lambda b, p, i: (b, p, 0, 0)),
        ],
        out_specs=pl.BlockSpec((1, tq, LANES), lambda b, p, i: (b, i, p)),
        out_shape=jax.ShapeDtypeStruct((bsz, seq, d), BF16),
        scratch_shapes=[
            pltpu.VMEM((2, head_dim + BF16_ROWS, tq), F32),
            pltpu.VMEM((2, tk, tq), F32),
        ],
        compiler_params=_cparams(("parallel", "parallel", "arbitrary")),
        name="fox_attention",
    )(q3, kaug3, v_t, c_row)


def _gdn_prep_kernel(p_ref, w_ref, o_ref, buf_ref, *, d_model, head_dim, q_scale):
    ts = p_ref.shape[1]
    pad = 8

    @pl.when(pl.program_id(1) == 0)
    def _():
        buf_ref[0:pad, :] = jnp.zeros((pad, buf_ref.shape[1]), F32)

    buf_ref[pad:pad + ts, :] = p_ref[0].astype(F32)
    for j in range(o_ref.shape[2] // LANES):
        sl = slice(j * LANES, (j + 1) * LANES)
        acc = buf_ref[pad:pad + ts, sl] * w_ref[CONV_WIDTH - 1:CONV_WIDTH, sl]
        for dlt in range(1, CONV_WIDTH):
            acc = acc + (buf_ref[pad - dlt:pad - dlt + ts, sl]
                         * w_ref[CONV_WIDTH - 1 - dlt:CONV_WIDTH - dlt, sl])
        y = _silu(acc)
        if j * LANES < 2 * d_model:
            y = y * lax.rsqrt(jnp.sum(y * y, axis=-1, keepdims=True) + L2_EPS)
            if j * LANES < d_model:
                y = y * q_scale
        o_ref[0, :, sl] = y.astype(BF16)
    buf_ref[0:pad, :] = buf_ref[ts:ts + pad, :]


def _gdn_prep(proj3, conv_w, d_model, head_dim):
    bsz, seq, _ = proj3.shape
    width = 3 * d_model
    ts = min(ROW_TILE, seq)
    return pl.pallas_call(
        functools.partial(_gdn_prep_kernel, d_model=d_model, head_dim=head_dim,
                          q_scale=head_dim ** -0.5),
        grid=(bsz, seq // ts),
        in_specs=[
            pl.BlockSpec((1, ts, width), lambda b, i: (b, i, 0)),
            pl.BlockSpec((CONV_WIDTH, width), lambda b, i: (0, 0)),
        ],
        out_specs=pl.BlockSpec((1, ts, width), lambda b, i: (b, i, 0)),
        out_shape=jax.ShapeDtypeStruct((bsz, seq, width), BF16),
        scratch_shapes=[pltpu.VMEM((ts + 8, width), F32)],
        compiler_params=_cparams(("parallel", "arbitrary")),
        name="gdn_prep",
    )(proj3, conv_w)


def _gdn_chunk_kernel(q_ref, k_ref, v_ref, col_ref, row_ref, o_ref, state_ref,
                      *, n_heads, head_dim):
    @pl.when(pl.program_id(1) == 0)
    def _():
        state_ref[...] = jnp.zeros(state_ref.shape, F32)

    n_chunks = q_ref.shape[1] // CHUNK
    ri = lax.broadcasted_iota(jnp.int32, (CHUNK, CHUNK), 0)
    ci = lax.broadcasted_iota(jnp.int32, (CHUNK, CHUNK), 1)
    eye = jnp.where(ri == ci, 1.0, 0.0)
    heads = range(n_heads)
    nt = (((1,), (1,)), ((), ()))

    def mm(a, b):
        return jnp.dot(a, b, preferred_element_type=F32)

    def stack(a, b):
        return jnp.concatenate([a, b], axis=0)

    def chunk(it, carry):
        first = it * GDN_UNROLL
        starts = [pl.multiple_of((first + cc) * CHUNK, CHUNK) for cc in range(GDN_UNROLL)]
        cols = [col_ref[0, first + cc] for cc in range(GDN_UNROLL)]
        rows = [row_ref[0, first + cc] for cc in range(GDN_UNROLL)]
        probs = range(GDN_UNROLL * n_heads)
        cc_of = [p // n_heads for p in probs]
        h_of = [p % n_heads for p in probs]
        sls = [slice(h * head_dim, (h + 1) * head_dim) for h in heads]
        q = [q_ref[0, pl.ds(starts[cc_of[p]], CHUNK), sls[h_of[p]]] for p in probs]
        k = [k_ref[0, pl.ds(starts[cc_of[p]], CHUNK), sls[h_of[p]]] for p in probs]
        v = [v_ref[0, pl.ds(starts[cc_of[p]], CHUNK), sls[h_of[p]]] for p in probs]
        beta = [cols[cc_of[p]][:, h_of[p]:h_of[p] + 1] for p in probs]
        g_col = [cols[cc_of[p]][:, n_heads + h_of[p]:n_heads + h_of[p] + 1] for p in probs]
        g_row = [rows[cc_of[p]][n_heads + h_of[p]:n_heads + h_of[p] + 1, :] for p in probs]
        g_last = [g_col[p][CHUNK - 1:CHUNK, :] for p in probs]
        decay = [jnp.exp(jnp.where(ri >= ci, g_col[p] - g_row[p], -jnp.inf)) for p in probs]
        kf = [k[p].astype(F32) for p in probs]
        kb = [kf[p] * beta[p] for p in probs]

        r1 = [lax.dot_general(stack(kb[p].astype(BF16), q[p]), k[p], nt,
                              preferred_element_type=F32) for p in probs]
        m = [jnp.where(ri > ci, r1[p][:CHUNK] * decay[p], 0.0) for p in probs]
        attn = [(r1[p][CHUNK:] * decay[p]).astype(BF16) for p in probs]

        t = [eye - m[p] for p in probs]
        pb = [m[p].astype(BF16) for p in probs]
        pw = [mm(pb[p], pb[p]) for p in probs]
        power = 2
        while power < CHUNK:
            pb = [pw[p].astype(BF16) for p in probs]
            if 2 * power >= CHUNK:
                t = [t[p] + mm(t[p].astype(BF16), pb[p]) for p in probs]
            else:
                r = [mm(stack(t[p].astype(BF16), pb[p]), pb[p]) for p in probs]
                t = [t[p] + r[p][:CHUNK] for p in probs]
                pw = [r[p][CHUNK:] for p in probs]
            power *= 2

        e_g = [jnp.exp(g_col[p]) for p in probs]
        rhs = [jnp.concatenate([v[p].astype(F32) * beta[p], kb[p] * e_g[p]], axis=1).astype(BF16)
               for p in probs]
        sol = [mm(t[p].astype(BF16), rhs[p]) for p in probs]
        q_dec = [(q[p].astype(F32) * e_g[p]).astype(BF16) for p in probs]
        k_dec_t = [(kf[p] * jnp.exp(g_last[p] - g_col[p])).T.astype(BF16) for p in probs]

        for cc in range(GDN_UNROLL):
            ps = [cc * n_heads + h for h in heads]
            sb = [state_ref[h].astype(BF16) for h in heads]
            r2 = [mm(stack(sol[ps[h]][:, head_dim:].astype(BF16), q_dec[ps[h]]), sb[h])
                  for h in heads]
            vb = [(sol[ps[h]][:, :head_dim] - r2[h][:CHUNK]).astype(BF16) for h in heads]
            r3 = [mm(stack(attn[ps[h]], k_dec_t[ps[h]]), vb[h]) for h in heads]
            for h in heads:
                o_ref[0, pl.ds(starts[cc], CHUNK), sls[h]] = (
                    r2[h][CHUNK:] + r3[h][:CHUNK]).astype(BF16)
                state_ref[h] = state_ref[h] * jnp.exp(g_last[ps[h]]) + r3[h][CHUNK:]
        return carry

    lax.fori_loop(0, n_chunks // GDN_UNROLL, chunk, 0)


def _gdn_delta_rule(qkv3, cols, rows, n_heads, head_dim):
    bsz, seq, _ = qkv3.shape
    d = n_heads * head_dim
    ts = min(GDN_TILE, seq)
    cpt = ts // CHUNK
    return pl.pallas_call(
        functools.partial(_gdn_chunk_kernel, n_heads=n_heads, head_dim=head_dim),
        grid=(bsz, seq // ts),
        in_specs=[
            pl.BlockSpec((1, ts, d), lambda b, i: (b, i, 0)),
            pl.BlockSpec((1, ts, d), lambda b, i: (b, i, 1)),
            pl.BlockSpec((1, ts, d), lambda b, i: (b, i, 2)),
            pl.BlockSpec((1, cpt, CHUNK, 2 * n_heads), lambda b, i: (b, i, 0, 0)),
            pl.BlockSpec((1, cpt, 2 * n_heads, CHUNK), lambda b, i: (b, i, 0, 0)),
        ],
        out_specs=pl.BlockSpec((1, ts, d), lambda b, i: (b, i, 0)),
        out_shape=jax.ShapeDtypeStruct((bsz, seq, d), BF16),
        scratch_shapes=[pltpu.VMEM((n_heads, head_dim, head_dim), F32)],
        compiler_params=_cparams(("parallel", "arbitrary")),
        name="gdn_delta_rule",
    )(qkv3, qkv3, qkv3, cols, rows)


def _route_rows(logits_t, bias):
    n_exp = logits_t.shape[0]
    mx = jnp.max(logits_t, axis=0, keepdims=True)
    ex = jnp.exp(logits_t - mx)
    probs = ex / jnp.sum(ex, axis=0, keepdims=True)
    sel = probs + bias
    sel_r = [sel[e:e + 1, :] for e in range(n_exp)]
    prob_r = [probs[e:e + 1, :] for e in range(n_exp)]

    def top2_sum(a, b, c, d):
        hi1, lo1 = jnp.maximum(a, b), jnp.minimum(a, b)
        hi2, lo2 = jnp.maximum(c, d), jnp.minimum(c, d)
        return jnp.maximum(hi1, hi2) + jnp.maximum(jnp.minimum(hi1, hi2), jnp.maximum(lo1, lo2))

    scores = [top2_sum(*sel_r[EXPERTS_PER_GROUP * g:EXPERTS_PER_GROUP * (g + 1)])
              for g in range(N_GROUPS)]
    best = scores[0]
    gidx = jnp.zeros(best.shape, jnp.int32)
    for g in range(1, N_GROUPS):
        upd = scores[g] > best
        best = jnp.where(upd, scores[g], best)
        gidx = jnp.where(upd, g, gidx)

    def pick(vals, j):
        out = vals[j]
        for g in range(1, N_GROUPS):
            out = jnp.where(gidx == g, vals[EXPERTS_PER_GROUP * g + j], out)
        return out

    cand = [pick(sel_r, j) for j in range(EXPERTS_PER_GROUP)]
    cprob = [pick(prob_r, j) for j in range(EXPERTS_PER_GROUP)]
    b1, i1, p1 = cand[0], jnp.zeros(best.shape, jnp.int32), cprob[0]
    for j in range(1, EXPERTS_PER_GROUP):
        upd = cand[j] > b1
        b1 = jnp.where(upd, cand[j], b1)
        i1 = jnp.where(upd, j, i1)
        p1 = jnp.where(upd, cprob[j], p1)
    b2 = jnp.full(best.shape, -jnp.inf, F32)
    i2 = jnp.full(best.shape, -1, jnp.int32)
    p2 = jnp.zeros(best.shape, F32)
    for j in range(EXPERTS_PER_GROUP):
        upd = jnp.logical_and(i1 != j, jnp.logical_or(cand[j] > b2, i2 < 0))
        b2 = jnp.where(upd, cand[j], b2)
        i2 = jnp.where(upd, j, i2)
        p2 = jnp.where(upd, cprob[j], p2)
    tot = p1 + p2
    base = gidx * EXPERTS_PER_GROUP
    return base + i1, base + i2, p1 / tot, p2 / tot


def _mix_out_kernel(o_ref, og_ref, h_ref, w_ref, lng_ref, lnb_ref, rw_ref, rb_ref,
                    *rest, alpha, gdn_head_dim):
    if gdn_head_dim:
        ng_ref, hn_ref, route_ref = rest
    else:
        hn_ref, route_ref = rest
    o = o_ref[...].astype(F32)
    og = og_ref[...].astype(F32)
    if gdn_head_dim:
        parts = []
        for j in range(o.shape[1] // gdn_head_dim):
            x = o[:, j * gdn_head_dim:(j + 1) * gdn_head_dim]
            r = lax.rsqrt(jnp.mean(x * x, axis=-1, keepdims=True) + RMS_EPS)
            parts.append(x * r * ng_ref[...])
        a = jnp.concatenate(parts, axis=1) * _silu(og)
    else:
        a = o * _sigmoid(og)
    mix = jnp.dot(a.astype(BF16), w_ref[...], preferred_element_type=F32)
    hn = _layer_norm_rows(alpha * h_ref[...] + mix, lng_ref[...], lnb_ref[...])
    hn_ref[...] = hn
    logits_t = lax.dot_general(rw_ref[...], hn, (((1,), (1,)), ((), ())),
                               preferred_element_type=F32,
                               precision=lax.Precision.HIGHEST)
    e1, e2, g1, g2 = _route_rows(logits_t, rb_ref[...])
    route_ref[0:1, :] = e1.astype(F32)
    route_ref[1:2, :] = e2.astype(F32)
    route_ref[2:3, :] = g1
    route_ref[3:4, :] = g2
    route_ref[4:8, :] = jnp.zeros((4, g1.shape[1]), F32)


def _mix_out(o, proj, og_col_block, h, w_out, ln_g, ln_b, rw_t, rb, alpha,
             norm_g=None, gdn_head_dim=0):
    n, d = o.shape
    n_exp = rw_t.shape[0]
    tm = min(ROW_TILE, n)
    row = lambda i: (i, 0)
    const = lambda i: (0, 0)
    in_specs = [
        pl.BlockSpec((tm, d), row),
        pl.BlockSpec((tm, d), lambda i: (i, og_col_block)),
        pl.BlockSpec((tm, d), row),
        pl.BlockSpec((d, d), const),
        pl.BlockSpec((1, d), const),
        pl.BlockSpec((1, d), const),
        pl.BlockSpec((n_exp, d), const),
        pl.BlockSpec((n_exp, 1), const),
    ]
    args = [o, proj, h, w_out, ln_g.reshape(1, d), ln_b.reshape(1, d), rw_t,
            rb.reshape(n_exp, 1)]
    if gdn_head_dim:
        in_specs.append(pl.BlockSpec((1, gdn_head_dim), const))
        args.append(norm_g.reshape(1, gdn_head_dim))
    return pl.pallas_call(
        functools.partial(_mix_out_kernel, alpha=alpha, gdn_head_dim=gdn_head_dim),
        grid=(n // tm,),
        in_specs=in_specs,
        out_specs=[pl.BlockSpec((tm, d), row), pl.BlockSpec((8, tm), lambda i: (0, i))],
        out_shape=[jax.ShapeDtypeStruct((n, d), F32), jax.ShapeDtypeStruct((8, n), F32)],
        compiler_params=_cparams(("parallel",)),
        name="mix_out",
    )(*args)


def _moe_kernel(elo_ref, ehi_ref, nblk_ref, src_ref, src_next_ref, dst_ref, gate_ref, h_hbm,
                wg_lo_ref, wu_lo_ref, wd_lo_ref, wg_hi_ref, wu_hi_ref, wd_hi_ref,
                lng_ref, lnb_ref, out_hbm,
                xbuf, ybuf, wgu_c, wd_c, gsem, ssem, *, alpha, d_expert, n_tok):
    i = pl.program_id(0)
    nblk = nblk_ref[0]
    slot = lax.rem(i, 2)
    blk = xbuf.shape[1]
    prev = jnp.maximum(i - 1, 0)

    def gather(idx_ref, dst):
        for r in range(blk):
            pltpu.make_async_copy(h_hbm.at[pl.ds(idx_ref[0, 0, r], 1)],
                                  xbuf.at[dst, pl.ds(r, 1)], gsem.at[dst]).start()

    def wait_gather(dst):
        pltpu.make_async_copy(h_hbm.at[pl.ds(0, blk)], xbuf.at[dst], gsem.at[dst]).wait()

    def wait_scatter(src):
        pltpu.make_async_copy(ybuf.at[src], out_hbm.at[pl.ds(0, blk)], ssem.at[src]).wait()

    @pl.when(i == 0)
    def _():
        ybuf[...] = jnp.zeros(ybuf.shape, F32)
        spare = [pltpu.make_async_copy(ybuf.at[s], out_hbm.at[pl.ds(n_tok + s * blk, blk)],
                                       ssem.at[s]) for s in range(2)]
        for cp in spare:
            cp.start()
        for cp in spare:
            cp.wait()
        gather(src_ref, 0)

    @pl.when(i + 1 < nblk)
    def _():
        gather(src_next_ref, 1 - slot)

    def refresh(which, e_ref, wg_ref, wu_ref, wd_ref):
        @pl.when(jnp.logical_or(i == 0, e_ref[i] != e_ref[prev]))
        def _():
            wgu_c[which, :, :d_expert] = wg_ref[0, 0].astype(BF16)
            wgu_c[which, :, d_expert:] = wu_ref[0, 0].astype(BF16)
            wd_c[which] = wd_ref[0, 0].astype(BF16)

    refresh(0, elo_ref, wg_lo_ref, wu_lo_ref, wd_lo_ref)
    refresh(1, ehi_ref, wg_hi_ref, wu_hi_ref, wd_hi_ref)

    @pl.when(i < nblk)
    def _():
        wait_gather(slot)
        x = xbuf[slot]
        xb = x.astype(BF16)

        def ffn(which):
            a = jnp.dot(xb, wgu_c[which], preferred_element_type=F32)
            hid = _silu(a[:, :d_expert]) * a[:, d_expert:]
            return jnp.dot(hid.astype(BF16), wd_c[which], preferred_element_type=F32)

        y = ffn(0) * gate_ref[:, 0:1] + ffn(1) * gate_ref[:, 1:2]
        res = _layer_norm_rows(alpha * x + y, lng_ref[...], lnb_ref[...])

        @pl.when(i >= 2)
        def _():
            wait_scatter(slot)

        ybuf[slot] = res
        for r in range(blk):
            pltpu.make_async_copy(ybuf.at[slot, pl.ds(r, 1)],
                                  out_hbm.at[pl.ds(dst_ref[0, 0, r], 1)], ssem.at[slot]).start()

        @pl.when(i == nblk - 1)
        def _():
            wait_scatter(slot)

            @pl.when(i >= 1)
            def _():
                wait_scatter(1 - slot)


def _moe_ffn(h, n_tok, src_idx, dst_idx, gates_sorted, blk_lo, blk_hi, n_used, wg, wu, wd,
             layer, ln_g, ln_b, alpha):
    d = h.shape[1]
    d_expert = wd.shape[2]
    lo_blk = lambda i, lo, hi, nb: (layer, lo[i], 0, 0)
    hi_blk = lambda i, lo, hi, nb: (layer, hi[i], 0, 0)
    n_blocks = src_idx.shape[0] // MOE_BLOCK
    src = src_idx.reshape(n_blocks, 1, MOE_BLOCK)
    dst = dst_idx.reshape(n_blocks, 1, MOE_BLOCK)
    last = n_blocks - 1
    smem_blk = functools.partial(pl.BlockSpec, (1, 1, MOE_BLOCK), memory_space=pltpu.SMEM)
    grid_spec = pltpu.PrefetchScalarGridSpec(
        num_scalar_prefetch=3,
        grid=(n_blocks,),
        in_specs=[
            smem_blk(lambda i, lo, hi, nb: (i, 0, 0)),
            smem_blk(lambda i, lo, hi, nb: (jnp.minimum(i + 1, last), 0, 0)),
            smem_blk(lambda i, lo, hi, nb: (i, 0, 0)),
            pl.BlockSpec((MOE_BLOCK, 2), lambda i, lo, hi, nb: (i, 0)),
            pl.BlockSpec(memory_space=pl.ANY),
            pl.BlockSpec((1, 1, d, d_expert), lo_blk),
            pl.BlockSpec((1, 1, d, d_expert), lo_blk),
            pl.BlockSpec((1, 1, d_expert, d), lo_blk),
            pl.BlockSpec((1, 1, d, d_expert), hi_blk),
            pl.BlockSpec((1, 1, d, d_expert), hi_blk),
            pl.BlockSpec((1, 1, d_expert, d), hi_blk),
            pl.BlockSpec((1, d), lambda i, lo, hi, nb: (0, 0)),
            pl.BlockSpec((1, d), lambda i, lo, hi, nb: (0, 0)),
        ],
        out_specs=pl.BlockSpec(memory_space=pl.ANY),
        scratch_shapes=[
            pltpu.VMEM((2, MOE_BLOCK, d), F32),
            pltpu.VMEM((2, MOE_BLOCK, d), F32),
            pltpu.VMEM((2, d, 2 * d_expert), BF16),
            pltpu.VMEM((2, d_expert, d), BF16),
            pltpu.SemaphoreType.DMA((2,)),
            pltpu.SemaphoreType.DMA((2,)),
        ],
    )
    return pl.pallas_call(
        functools.partial(_moe_kernel, alpha=alpha, d_expert=d_expert, n_tok=n_tok),
        grid_spec=grid_spec,
        out_shape=jax.ShapeDtypeStruct((n_tok + MOE_SPARE_ROWS, d), F32),
        compiler_params=_cparams(("arbitrary",)),
        name="moe_ffn",
    )(blk_lo, blk_hi, n_used, src, src, dst, gates_sorted, h, wg, wu, wd, wg, wu, wd,
      ln_g.reshape(1, d), ln_b.reshape(1, d))


def _dispatch_plan(route, n_tok):
    e1 = route[0].astype(jnp.int32)
    e2 = route[1].astype(jnp.int32)
    first_is_lo = e1 < e2
    lo = jnp.where(first_is_lo, e1, e2)
    hi = jnp.where(first_is_lo, e2, e1)
    g_lo = jnp.where(first_is_lo, route[2], route[3])
    g_hi = jnp.where(first_is_lo, route[3], route[2])
    grp = lo // EXPERTS_PER_GROUP
    a = lo % EXPERTS_PER_GROUP
    b = hi % EXPERTS_PER_GROUP
    cls = grp * PAIRS_PER_GROUP + (a * (7 - a)) // 2 + (b - a - 1)
    n_cls = N_GROUPS * PAIRS_PER_GROUP
    n_blocks = -(-(n_tok + n_cls * (MOE_BLOCK - 1)) // MOE_BLOCK)
    n_slots = n_blocks * MOE_BLOCK

    counts = jnp.bincount(cls, length=n_cls)
    start = jnp.cumsum(counts) - counts
    padded = (counts + MOE_BLOCK - 1) // MOE_BLOCK * MOE_BLOCK
    pend = jnp.cumsum(padded)
    pstart = pend - padded
    order = jnp.argsort(cls, stable=True).astype(jnp.int32)
    slot_ids = jnp.arange(n_slots)
    slot_cls = jnp.minimum(jnp.sum(slot_ids[:, None] >= pend[None, :], axis=1), n_cls - 1)
    rank = slot_ids - pstart[slot_cls]
    valid = rank < counts[slot_cls]
    slot_tok = jnp.where(valid, order[jnp.clip(start[slot_cls] + rank, 0, n_tok - 1)], 0)
    slot_tok = slot_tok.astype(jnp.int32)
    spare = n_tok + (slot_ids // MOE_BLOCK % 2) * MOE_BLOCK + slot_ids % MOE_BLOCK
    slot_dst = jnp.where(valid, slot_tok, spare).astype(jnp.int32)
    blk_cls = slot_cls[::MOE_BLOCK]
    pair_lo = jnp.array([0, 0, 0, 1, 1, 2], jnp.int32)
    pair_hi = jnp.array([1, 2, 3, 2, 3, 3], jnp.int32)
    blk_grp = blk_cls // PAIRS_PER_GROUP
    blk_pair = blk_cls % PAIRS_PER_GROUP
    blk_lo = (blk_grp * EXPERTS_PER_GROUP + pair_lo[blk_pair]).astype(jnp.int32)
    blk_hi = (blk_grp * EXPERTS_PER_GROUP + pair_hi[blk_pair]).astype(jnp.int32)
    n_used = (pend[-1] // MOE_BLOCK).astype(jnp.int32).reshape(1)
    gates_sorted = jnp.stack([g_lo[slot_tok], g_hi[slot_tok]], axis=1)
    return slot_tok, slot_dst, gates_sorted, blk_lo, blk_hi, n_used


def _moe_layer(h, route, wg, wu, wd, layer, ln_g, ln_b, alpha):
    n_tok = h.shape[0]
    slot_tok, slot_dst, gates_sorted, blk_lo, blk_hi, n_used = _dispatch_plan(route, n_tok)
    return _moe_ffn(h, n_tok, slot_tok, slot_dst, gates_sorted, blk_lo, blk_hi, n_used,
                    wg, wu, wd, layer, ln_g, ln_b, alpha)


def _pad_cols(w, width):
    return jnp.pad(w, ((0, 0), (0, width - w.shape[1])))


def kernel(x, ln_mix_g, ln_mix_b, ln_ffn_g, ln_ffn_b, router_w, router_b, fox_w_in, fox_b_f,
           fox_q_gain, fox_k_gain, fox_w_out, gdn_w_in, gdn_conv_w, gdn_a_log, gdn_dt_bias,
           gdn_norm_g, gdn_w_out, moe_w_gate, moe_w_up, moe_w_down):
    bsz, seq, d = x.shape
    n = bsz * seq
    depth = ln_mix_g.shape[0]
    alpha = (2.0 * depth) ** 0.25
    fox_heads = fox_b_f.shape[1]
    fox_dim = d // fox_heads
    gdn_heads = gdn_a_log.shape[1]
    gdn_dim = d // gdn_heads
    rw_t = router_w.T

    h = x.reshape(n, d)
    for i in range(depth):
        j = i // 2
        if i % 2 == 0:
            w_in = fox_w_in[j]
            proj, side = _inproj(h, n, w_in[:, :4 * d].astype(BF16),
                                 _pad_cols(w_in[:, 4 * d:], LANES).astype(BF16))
            gains = jnp.concatenate([jnp.tile(fox_q_gain[j], fox_heads) * (fox_dim ** -0.5 * LOG2E),
                                     jnp.tile(fox_k_gain[j], fox_heads)]).reshape(1, 2 * d)
            f_t = side[:, :fox_heads].reshape(bsz, seq, fox_heads).transpose(0, 2, 1)
            c = _fox_gates(f_t, fox_b_f[j])
            c_row = c.reshape(bsz, fox_heads // 2, 2, seq)
            q3, kaug3, v_t = _fox_prep(proj.reshape(bsz, seq, 4 * d), gains,
                                       c.transpose(0, 2, 1), fox_dim)
            o = _fox_attention(q3, kaug3, v_t, c_row, fox_heads, fox_dim)
            h, route = _mix_out(o.reshape(n, d), proj, 3, h, fox_w_out[j].astype(BF16),
                                ln_mix_g[i], ln_mix_b[i], rw_t, router_b, alpha)
        else:
            w_in = gdn_w_in[j]
            proj, side = _inproj(h, n, w_in[:, :4 * d].astype(BF16),
                                 _pad_cols(w_in[:, 4 * d:], LANES).astype(BF16))
            qkv = _gdn_prep(proj.reshape(bsz, seq, 4 * d), gdn_conv_w[j], d, gdn_dim)
            side3 = side.reshape(bsz, seq, LANES)
            b_t = side3[:, :, :gdn_heads].transpose(0, 2, 1)
            a_t = side3[:, :, gdn_heads:2 * gdn_heads].transpose(0, 2, 1)
            beta, gc = _gdn_gates(b_t, a_t, gdn_a_log[j], gdn_dt_bias[j])
            rows = jnp.concatenate([beta, gc], axis=1).reshape(
                bsz, 2 * gdn_heads, seq // CHUNK, CHUNK).transpose(0, 2, 1, 3)
            cols = rows.transpose(0, 1, 3, 2)
            o = _gdn_delta_rule(qkv, cols, rows, gdn_heads, gdn_dim)
            h, route = _mix_out(o.reshape(n, d), proj, 3, h, gdn_w_out[j].astype(BF16),
                                ln_mix_g[i], ln_mix_b[i], rw_t, router_b, alpha,
                                norm_g=gdn_norm_g[j], gdn_head_dim=gdn_dim)
        h = _moe_layer(h, route, moe_w_gate, moe_w_up, moe_w_down, i,
                       ln_ffn_g[i], ln_ffn_b[i], alpha)
    return h[:n].reshape(bsz, seq, d)
```

```python
import functools
import math

import jax
import jax.numpy as jnp
from jax import lax
from jax.experimental import pallas as pl
from jax.experimental.pallas import tpu as pltpu

F32 = jnp.float32
BF16 = jnp.bfloat16

N_GROUPS = 4
EXPERTS_PER_GROUP = 4
PAIRS_PER_GROUP = 6
CHUNK = 64
CONV_WIDTH = 4
LN_EPS = 1e-5
RMS_EPS = 1e-6
L2_EPS = 1e-6

LOG2E = math.log2(math.e)

LANES = 128
BF16_ROWS = 16
VMEM_LIMIT = 56 * 1024 * 1024

ROW_TILE = 512
ATT_Q_TILE = 512
GDN_TILE = 256
GDN_UNROLL = 2
MOE_BLOCK = 256
MOE_SPARE_ROWS = 2 * MOE_BLOCK


def _cparams(sem):
    return pltpu.CompilerParams(dimension_semantics=sem, vmem_limit_bytes=VMEM_LIMIT)


def _layer_norm_rows(y, g, b):
    mu = jnp.mean(y, axis=-1, keepdims=True)
    yc = y - mu
    var = jnp.mean(yc * yc, axis=-1, keepdims=True)
    return yc * lax.rsqrt(var + LN_EPS) * g + b


def _sigmoid(x):
    return 1.0 / (1.0 + jnp.exp(-x))


def _silu(x):
    return x * _sigmoid(x)


def _softplus(x):
    return jnp.maximum(x, 0.0) + jnp.log(1.0 + jnp.exp(-jnp.abs(x)))


def _inproj_fox_kernel(x_ref, w_ref, ws_ref, gain_ref, qk_ref, vt_ref, og_ref, os_ref,
                       *, head_dim):
    xb = x_ref[...].astype(BF16)
    d = og_ref.shape[1]
    lane = lax.broadcasted_iota(jnp.int32, (1, LANES), 1)
    first = lane < head_dim

    def part(p):
        return jnp.dot(xb, w_ref[:, p * d:(p + 1) * d], preferred_element_type=F32)

    for p in range(2):
        acc = part(p)
        for j in range(d // LANES):
            x = acc[:, j * LANES:(j + 1) * LANES]
            xx = x * x
            s_a = jnp.sum(jnp.where(first, xx, 0.0), axis=-1, keepdims=True)
            s_b = jnp.sum(jnp.where(first, 0.0, xx), axis=-1, keepdims=True)
            r = jnp.where(first, lax.rsqrt(s_a / head_dim + RMS_EPS),
                          lax.rsqrt(s_b / head_dim + RMS_EPS))
            sl = slice(p * d + j * LANES, p * d + (j + 1) * LANES)
            qk_ref[0, :, sl] = (x * r * gain_ref[:, sl]).astype(BF16)
    acc = part(2)
    for j in range(d // LANES):
        vt_ref[0, j * LANES:(j + 1) * LANES, :] = acc[:, j * LANES:(j + 1) * LANES].T.astype(BF16)
    og_ref[...] = part(3).astype(BF16)
    os_ref[...] = jnp.dot(xb, ws_ref[...], preferred_element_type=F32)


def _inproj_fox(x, bsz, seq, w_main, w_side, gains, head_dim):
    d = x.shape[1]
    n = bsz * seq
    ts = min(ROW_TILE, seq)
    tiles = seq // ts
    row = lambda b, i: (b * tiles + i, 0)
    const = lambda b, i: (0, 0)
    return pl.pallas_call(
        functools.partial(_inproj_fox_kernel, head_dim=head_dim),
        grid=(bsz, tiles),
        in_specs=[
            pl.BlockSpec((ts, d), row),
            pl.BlockSpec((d, 4 * d), const),
            pl.BlockSpec((d, LANES), const),
            pl.BlockSpec((1, 2 * d), const),
        ],
        out_specs=[
            pl.BlockSpec((1, ts, 2 * d), lambda b, i: (b, i, 0)),
            pl.BlockSpec((1, d, ts), lambda b, i: (b, 0, i)),
            pl.BlockSpec((ts, d), row),
            pl.BlockSpec((ts, LANES), row),
        ],
        out_shape=[
            jax.ShapeDtypeStruct((bsz, seq, 2 * d), BF16),
            jax.ShapeDtypeStruct((bsz, d, seq), BF16),
            jax.ShapeDtypeStruct((n, d), BF16),
            jax.ShapeDtypeStruct((n, LANES), F32),
        ],
        compiler_params=_cparams(("parallel", "parallel")),
        name="inproj_fox",
    )(x, w_main, w_side, gains)


def _inproj_gdn_kernel(x_ref, w_ref, ws_ref, cw_ref, qkv_ref, og_ref, os_ref, buf_ref,
                       *, q_scale):
    ts = x_ref.shape[0]
    d = og_ref.shape[1]
    pad = 8

    @pl.when(pl.program_id(1) == 0)
    def _():
        buf_ref[0:pad, :] = jnp.zeros((pad, buf_ref.shape[1]), F32)

    xb = x_ref[...].astype(BF16)

    def project(p):
        buf_ref[pad:pad + ts, p * d:(p + 1) * d] = jnp.dot(
            xb, w_ref[:, p * d:(p + 1) * d], preferred_element_type=F32)

    def finish(p):
        for j in range(p * d // LANES, (p + 1) * d // LANES):
            sl = slice(j * LANES, (j + 1) * LANES)
            acc = buf_ref[pad:pad + ts, sl] * cw_ref[CONV_WIDTH - 1:CONV_WIDTH, sl]
            for dlt in range(1, CONV_WIDTH):
                acc = acc + (buf_ref[pad - dlt:pad - dlt + ts, sl]
                             * cw_ref[CONV_WIDTH - 1 - dlt:CONV_WIDTH - dlt, sl])
            y = _silu(acc)
            if p < 2:
                y = y * lax.rsqrt(jnp.sum(y * y, axis=-1, keepdims=True) + L2_EPS)
                if p == 0:
                    y = y * q_scale
            qkv_ref[0, :, sl] = y.astype(BF16)

    project(0)
    project(1)
    finish(0)
    project(2)
    finish(1)
    og_ref[...] = jnp.dot(xb, w_ref[:, 3 * d:4 * d], preferred_element_type=F32).astype(BF16)
    finish(2)
    os_ref[...] = jnp.dot(xb, ws_ref[...], preferred_element_type=F32)
    buf_ref[0:pad, :] = buf_ref[ts:ts + pad, :]


def _inproj_gdn(x, bsz, seq, w_main, w_side, conv_w, head_dim):
    d = x.shape[1]
    n = bsz * seq
    ts = min(ROW_TILE, seq)
    tiles = seq // ts
    row = lambda b, i: (b * tiles + i, 0)
    const = lambda b, i: (0, 0)
    return pl.pallas_call(
        functools.partial(_inproj_gdn_kernel, q_scale=head_dim ** -0.5),
        grid=(bsz, tiles),
        in_specs=[
            pl.BlockSpec((ts, d), row),
            pl.BlockSpec((d, 4 * d), const),
            pl.BlockSpec((d, LANES), const),
            pl.BlockSpec((CONV_WIDTH, 3 * d), const),
        ],
        out_specs=[
            pl.BlockSpec((1, ts, 3 * d), lambda b, i: (b, i, 0)),
            pl.BlockSpec((ts, d), row),
            pl.BlockSpec((ts, LANES), row),
        ],
        out_shape=[
            jax.ShapeDtypeStruct((bsz, seq, 3 * d), BF16),
            jax.ShapeDtypeStruct((n, d), BF16),
            jax.ShapeDtypeStruct((n, LANES), F32),
        ],
        scratch_shapes=[pltpu.VMEM((ts + 8, 3 * d), F32)],
        compiler_params=_cparams(("parallel", "arbitrary")),
        name="inproj_gdn",
    )(x, w_main, w_side, conv_w)


def _prefix_sum_lanes(x, period):
    pos = lax.broadcasted_iota(jnp.int32, x.shape, 1) % period
    d = 1
    while d < period:
        x = x + jnp.where(pos >= d, pltpu.roll(x, d, axis=1), 0.0)
        d *= 2
    return x


def _fox_gate_kernel(f_ref, b_ref, c_ref):
    z = f_ref[0] + b_ref[...]
    log_f = -_softplus(-z)
    c_ref[0] = _prefix_sum_lanes(log_f, z.shape[1])


def _fox_gates(f_logit_t, b_f):
    bsz, h, s = f_logit_t.shape
    return pl.pallas_call(
        _fox_gate_kernel,
        grid=(bsz,),
        in_specs=[pl.BlockSpec((1, h, s), lambda b: (b, 0, 0)),
                  pl.BlockSpec((h, 1), lambda b: (0, 0))],
        out_specs=pl.BlockSpec((1, h, s), lambda b: (b, 0, 0)),
        out_shape=jax.ShapeDtypeStruct((bsz, h, s), F32),
        compiler_params=_cparams(("parallel",)),
        name="fox_gates",
    )(f_logit_t, b_f.reshape(h, 1))


def _gdn_gate_kernel(b_ref, a_ref, alog_ref, dt_ref, beta_ref, gc_ref):
    beta_ref[0] = _sigmoid(b_ref[0])
    g = -jnp.exp(alog_ref[...]) * _softplus(a_ref[0] + dt_ref[...])
    gc_ref[0] = _prefix_sum_lanes(g, CHUNK)


def _gdn_gates(b_logit_t, a_logit_t, a_log, dt_bias):
    bsz, h, s = b_logit_t.shape
    spec = pl.BlockSpec((1, h, s), lambda b: (b, 0, 0))
    vec = pl.BlockSpec((h, 1), lambda b: (0, 0))
    return pl.pallas_call(
        _gdn_gate_kernel,
        grid=(bsz,),
        in_specs=[spec, spec, vec, vec],
        out_specs=[spec, spec],
        out_shape=[jax.ShapeDtypeStruct((bsz, h, s), F32)] * 2,
        compiler_params=_cparams(("parallel",)),
        name="gdn_gates",
    )(b_logit_t, a_logit_t, a_log.reshape(h, 1), dt_bias.reshape(h, 1))


def _fox_attn_kernel(q_ref, k_ref, vt_ref, crow_ref, o_ref, ckb_ref, acc_ref, sbuf_ref,
                     *, head_dim, tq, tk):
    qi = pl.program_id(2)
    seq = k_ref.shape[1]

    @pl.when(qi == 0)
    def _():
        def fill(j, carry):
            st = pl.multiple_of(j * LANES, LANES)
            rows = crow_ref[0, 0, :, pl.ds(st, LANES)] * LOG2E
            for hh in range(2):
                ckb_ref[hh, pl.ds(st, LANES), :] = jnp.broadcast_to(
                    rows[hh:hh + 1, :], (LANES, LANES)).T
            return carry
        lax.fori_loop(0, seq // LANES, fill, 0)

    lane = lax.broadcasted_iota(jnp.int32, (1, LANES), 1)
    first = lane < head_dim
    q = q_ref[0]
    zero = jnp.zeros_like(q)
    q_heads = (jnp.where(first, q, zero), jnp.where(first, zero, q))
    q_start = pl.multiple_of(qi * tq, tq)
    c_q = crow_ref[0, 0, :, pl.ds(q_start, tq)] * LOG2E
    acc_ref[...] = jnp.zeros(acc_ref.shape, F32)
    ones = jnp.ones((acc_ref.shape[1] - head_dim, tk), BF16)

    def scores(j):
        start = pl.multiple_of(j * tk, tk)
        k = k_ref[0, pl.ds(start, tk), :]
        return [lax.dot_general(k, q_heads[hh], (((1,), (1,)), ((), ())),
                                preferred_element_type=F32) for hh in range(2)]

    def update(j, s_in, stats, masked):
        start = pl.multiple_of(j * tk, tk)
        out = []
        for hh in range(2):
            m_old = stats[hh]
            s0 = s_in[hh]
            cb = ckb_ref[hh, pl.ds(start, tk), :]
            slabs = []
            for t in range(tq // LANES):
                sl = s0[:, t * LANES:(t + 1) * LANES] - cb
                if masked:
                    kpos = start + lax.broadcasted_iota(jnp.int32, sl.shape, 0)
                    qpos = q_start + t * LANES + lax.broadcasted_iota(jnp.int32, sl.shape, 1)
                    sl = jnp.where(kpos <= qpos, sl, -jnp.inf)
                slabs.append(sl)
            s = jnp.concatenate(slabs, axis=1)
            cq = c_q[hh:hh + 1, :]
            m_new = jnp.maximum(m_old, jnp.max(s, axis=0, keepdims=True) + cq)
            p = jnp.exp2(s + (cq - m_new))
            alpha = jnp.exp2(m_old - m_new)
            vt = jnp.concatenate(
                [vt_ref[0, hh * head_dim:(hh + 1) * head_dim, pl.ds(start, tk)], ones], axis=0)
            pv = jnp.dot(vt, p.astype(BF16), preferred_element_type=F32)
            acc_ref[hh] = acc_ref[hh] * alpha + pv
            out.append(m_new)
        return tuple(out)

    def park(s):
        sbuf_ref[0] = s[0]
        sbuf_ref[1] = s[1]

    def parked():
        return [sbuf_ref[0], sbuf_ref[1]]

    def pair(i, stats):
        j0 = 2 * i
        s1 = scores(j0 + 1)
        stats = update(j0, parked(), stats, False)
        park(scores(j0 + 2))
        return update(j0 + 1, s1, stats, False)

    neg = jnp.full((1, tq), -1e30, F32)
    park(scores(0))
    stats = lax.fori_loop(0, qi, pair, (neg, neg))
    j0 = 2 * qi
    s1 = scores(j0 + 1)
    stats = update(j0, parked(), stats, True)
    update(j0 + 1, s1, stats, True)
    o_t = jnp.concatenate(
        [acc_ref[hh, :head_dim, :] / acc_ref[hh, head_dim:head_dim + 1, :] for hh in range(2)],
        axis=0)
    o_ref[0] = o_t.T.astype(BF16)


def _fox_attention(qk3, v_t, c_row, n_heads, head_dim):
    bsz, seq, _ = qk3.shape
    d = n_heads * head_dim
    pairs = d // LANES
    tq = min(ATT_Q_TILE, seq)
    tk = tq // 2
    assert seq % tq == 0
    return pl.pallas_call(
        functools.partial(_fox_attn_kernel, head_dim=head_dim, tq=tq, tk=tk),
        grid=(bsz, pairs, seq // tq),
        in_specs=[
            pl.BlockSpec((1, tq, LANES), lambda b, p, i: (b, i, p)),
            pl.BlockSpec((1, seq, LANES), lambda b, p, i: (b, 0, pairs + p)),
            pl.BlockSpec((1, LANES, seq), lambda b, p, i: (b, p, 0)),
            pl.BlockSpec((1, 1, 2, seq), lambda b, p, i: (b, p, 0, 0)),
        ],
        out_specs=pl.BlockSpec((1, tq, LANES), lambda b, p, i: (b, i, p)),
        out_shape=jax.ShapeDtypeStruct((bsz, seq, d), BF16),
        scratch_shapes=[
            pltpu.VMEM((2, seq, LANES), F32),
            pltpu.VMEM((2, head_dim + BF16_ROWS, tq), F32),
            pltpu.VMEM((2, tk, tq), F32),
        ],
        compiler_params=_cparams(("parallel", "parallel", "arbitrary")),
        name="fox_attention",
    )(qk3, qk3, v_t, c_row)


def _gdn_chunk_kernel(q_ref, k_ref, v_ref, col_ref, row_ref, o_ref, state_ref,
                      *, n_heads, head_dim):
    @pl.when(pl.program_id(1) == 0)
    def _():
        state_ref[...] = jnp.zeros(state_ref.shape, F32)

    n_chunks = q_ref.shape[1] // CHUNK
    ri = lax.broadcasted_iota(jnp.int32, (CHUNK, CHUNK), 0)
    ci = lax.broadcasted_iota(jnp.int32, (CHUNK, CHUNK), 1)
    eye = jnp.where(ri == ci, 1.0, 0.0)
    heads = range(n_heads)
    nt = (((1,), (1,)), ((), ()))

    def mm(a, b):
        return jnp.dot(a, b, preferred_element_type=F32)

    def stack(a, b):
        return jnp.concatenate([a, b], axis=0)

    def chunk(it, carry):
        first = it * GDN_UNROLL
        starts = [pl.multiple_of((first + cc) * CHUNK, CHUNK) for cc in range(GDN_UNROLL)]
        cols = [col_ref[0, first + cc] for cc in range(GDN_UNROLL)]
        rows = [row_ref[0, first + cc] for cc in range(GDN_UNROLL)]
        probs = range(GDN_UNROLL * n_heads)
        cc_of = [p // n_heads for p in probs]
        h_of = [p % n_heads for p in probs]
        sls = [slice(h * head_dim, (h + 1) * head_dim) for h in heads]
        q = [q_ref[0, pl.ds(starts[cc_of[p]], CHUNK), sls[h_of[p]]] for p in probs]
        k = [k_ref[0, pl.ds(starts[cc_of[p]], CHUNK), sls[h_of[p]]] for p in probs]
        v = [v_ref[0, pl.ds(starts[cc_of[p]], CHUNK), sls[h_of[p]]] for p in probs]
        beta = [cols[cc_of[p]][:, h_of[p]:h_of[p] + 1] for p in probs]
        g_col = [cols[cc_of[p]][:, n_heads + h_of[p]:n_heads + h_of[p] + 1] for p in probs]
        g_row = [rows[cc_of[p]][n_heads + h_of[p]:n_heads + h_of[p] + 1, :] for p in probs]
        g_last = [g_col[p][CHUNK - 1:CHUNK, :] for p in probs]
        decay = [jnp.exp(jnp.where(ri >= ci, g_col[p] - g_row[p], -jnp.inf)) for p in probs]
        kf = [k[p].astype(F32) for p in probs]
        kb = [kf[p] * beta[p] for p in probs]

        r1 = [lax.dot_general(stack(kb[p].astype(BF16), q[p]), k[p], nt,
                              preferred_element_type=F32) for p in probs]
        m = [jnp.where(ri > ci, r1[p][:CHUNK] * decay[p], 0.0) for p in probs]
        attn = [(r1[p][CHUNK:] * decay[p]).astype(BF16) for p in probs]

        t = [eye - m[p] for p in probs]
        pb = [m[p].astype(BF16) for p in probs]
        pw = [mm(pb[p], pb[p]) for p in probs]
        power = 2
        while power < CHUNK:
            pb = [pw[p].astype(BF16) for p in probs]
            if 2 * power >= CHUNK:
                t = [t[p] + mm(t[p].astype(BF16), pb[p]) for p in probs]
            else:
                r = [mm(stack(t[p].astype(BF16), pb[p]), pb[p]) for p in probs]
                t = [t[p] + r[p][:CHUNK] for p in probs]
                pw = [r[p][CHUNK:] for p in probs]
            power *= 2

        e_g = [jnp.exp(g_col[p]) for p in probs]
        rhs = [jnp.concatenate([v[p].astype(F32) * beta[p], kb[p] * e_g[p]], axis=1).astype(BF16)
               for p in probs]
        sol = [mm(t[p].astype(BF16), rhs[p]) for p in probs]
        q_dec = [(q[p].astype(F32) * e_g[p]).astype(BF16) for p in probs]
        k_dec_t = [(kf[p] * jnp.exp(g_last[p] - g_col[p])).T.astype(BF16) for p in probs]

        for cc in range(GDN_UNROLL):
            ps = [cc * n_heads + h for h in heads]
            sb = [state_ref[h].astype(BF16) for h in heads]
            r2 = [mm(stack(sol[ps[h]][:, head_dim:].astype(BF16), q_dec[ps[h]]), sb[h])
                  for h in heads]
            vb = [(sol[ps[h]][:, :head_dim] - r2[h][:CHUNK]).astype(BF16) for h in heads]
            r3 = [mm(stack(attn[ps[h]], k_dec_t[ps[h]]), vb[h]) for h in heads]
            for h in heads:
                o_ref[0, pl.ds(starts[cc], CHUNK), sls[h]] = (
                    r2[h][CHUNK:] + r3[h][:CHUNK]).astype(BF16)
                state_ref[h] = state_ref[h] * jnp.exp(g_last[ps[h]]) + r3[h][CHUNK:]
        return carry

    lax.fori_loop(0, n_chunks // GDN_UNROLL, chunk, 0)


def _gdn_delta_rule(qkv3, cols, rows, n_heads, head_dim):
    bsz, seq, _ = qkv3.shape
    d = n_heads * head_dim
    ts = min(GDN_TILE, seq)
    cpt = ts // CHUNK
    return pl.pallas_call(
        functools.partial(_gdn_chunk_kernel, n_heads=n_heads, head_dim=head_dim),
        grid=(bsz, seq // ts),
        in_specs=[
            pl.BlockSpec((1, ts, d), lambda b, i: (b, i, 0)),
            pl.BlockSpec((1, ts, d), lambda b, i: (b, i, 1)),
            pl.BlockSpec((1, ts, d), lambda b, i: (b, i, 2)),
            pl.BlockSpec((1, cpt, CHUNK, 2 * n_heads), lambda b, i: (b, i, 0, 0)),
            pl.BlockSpec((1, cpt, 2 * n_heads, CHUNK), lambda b, i: (b, i, 0, 0)),
        ],
        out_specs=pl.BlockSpec((1, ts, d), lambda b, i: (b, i, 0)),
        out_shape=jax.ShapeDtypeStruct((bsz, seq, d), BF16),
        scratch_shapes=[pltpu.VMEM((n_heads, head_dim, head_dim), F32)],
        compiler_params=_cparams(("parallel", "arbitrary")),
        name="gdn_delta_rule",
    )(qkv3, qkv3, qkv3, cols, rows)


def _route_rows(logits_t, bias):
    n_exp = logits_t.shape[0]
    mx = jnp.max(logits_t, axis=0, keepdims=True)
    ex = jnp.exp(logits_t - mx)
    probs = ex / jnp.sum(ex, axis=0, keepdims=True)
    sel = probs + bias
    sel_r = [sel[e:e + 1, :] for e in range(n_exp)]
    prob_r = [probs[e:e + 1, :] for e in range(n_exp)]

    def top2_sum(a, b, c, d):
        hi1, lo1 = jnp.maximum(a, b), jnp.minimum(a, b)
        hi2, lo2 = jnp.maximum(c, d), jnp.minimum(c, d)
        return jnp.maximum(hi1, hi2) + jnp.maximum(jnp.minimum(hi1, hi2), jnp.maximum(lo1, lo2))

    scores = [top2_sum(*sel_r[EXPERTS_PER_GROUP * g:EXPERTS_PER_GROUP * (g + 1)])
              for g in range(N_GROUPS)]
    best = scores[0]
    gidx = jnp.zeros(best.shape, jnp.int32)
    for g in range(1, N_GROUPS):
        upd = scores[g] > best
        best = jnp.where(upd, scores[g], best)
        gidx = jnp.where(upd, g, gidx)

    def pick(vals, j):
        out = vals[j]
        for g in range(1, N_GROUPS):
            out = jnp.where(gidx == g, vals[EXPERTS_PER_GROUP * g + j], out)
        return out

    cand = [pick(sel_r, j) for j in range(EXPERTS_PER_GROUP)]
    cprob = [pick(prob_r, j) for j in range(EXPERTS_PER_GROUP)]
    b1, i1, p1 = cand[0], jnp.zeros(best.shape, jnp.int32), cprob[0]
    for j in range(1, EXPERTS_PER_GROUP):
        upd = cand[j] > b1
        b1 = jnp.where(upd, cand[j], b1)
        i1 = jnp.where(upd, j, i1)
        p1 = jnp.where(upd, cprob[j], p1)
    b2 = jnp.full(best.shape, -jnp.inf, F32)
    i2 = jnp.full(best.shape, -1, jnp.int32)
    p2 = jnp.zeros(best.shape, F32)
    for j in range(EXPERTS_PER_GROUP):
        upd = jnp.logical_and(i1 != j, jnp.logical_or(cand[j] > b2, i2 < 0))
        b2 = jnp.where(upd, cand[j], b2)
        i2 = jnp.where(upd, j, i2)
        p2 = jnp.where(upd, cprob[j], p2)
    tot = p1 + p2
    base = gidx * EXPERTS_PER_GROUP
    return base + i1, base + i2, p1 / tot, p2 / tot


def _mix_out_kernel(o_ref, og_ref, h_ref, w_ref, lng_ref, lnb_ref, rw_ref, rb_ref,
                    *rest, alpha, gdn_head_dim):
    if gdn_head_dim:
        ng_ref, hn_ref, route_ref = rest
    else:
        hn_ref, route_ref = rest
    o = o_ref[...].astype(F32)
    og = og_ref[...].astype(F32)
    if gdn_head_dim:
        parts = []
        for j in range(o.shape[1] // gdn_head_dim):
            x = o[:, j * gdn_head_dim:(j + 1) * gdn_head_dim]
            r = lax.rsqrt(jnp.mean(x * x, axis=-1, keepdims=True) + RMS_EPS)
            parts.append(x * r * ng_ref[...])
        a = jnp.concatenate(parts, axis=1) * _silu(og)
    else:
        a = o * _sigmoid(og)
    mix = jnp.dot(a.astype(BF16), w_ref[...], preferred_element_type=F32)
    hn = _layer_norm_rows(alpha * h_ref[...] + mix, lng_ref[...], lnb_ref[...])
    hn_ref[...] = hn
    logits_t = lax.dot_general(rw_ref[...], hn, (((1,), (1,)), ((), ())),
                               preferred_element_type=F32,
                               precision=lax.Precision.HIGHEST)
    e1, e2, g1, g2 = _route_rows(logits_t, rb_ref[...])
    route_ref[0:1, :] = e1.astype(F32)
    route_ref[1:2, :] = e2.astype(F32)
    route_ref[2:3, :] = g1
    route_ref[3:4, :] = g2
    route_ref[4:8, :] = jnp.zeros((4, g1.shape[1]), F32)


def _mix_out(o, og, h, w_out, ln_g, ln_b, rw_t, rb, alpha,
             norm_g=None, gdn_head_dim=0):
    n, d = o.shape
    n_exp = rw_t.shape[0]
    tm = min(ROW_TILE, n)
    row = lambda i: (i, 0)
    const = lambda i: (0, 0)
    in_specs = [
        pl.BlockSpec((tm, d), row),
        pl.BlockSpec((tm, d), row),
        pl.BlockSpec((tm, d), row),
        pl.BlockSpec((d, d), const),
        pl.BlockSpec((1, d), const),
        pl.BlockSpec((1, d), const),
        pl.BlockSpec((n_exp, d), const),
        pl.BlockSpec((n_exp, 1), const),
    ]
    args = [o, og, h, w_out, ln_g.reshape(1, d), ln_b.reshape(1, d), rw_t,
            rb.reshape(n_exp, 1)]
    if gdn_head_dim:
        in_specs.append(pl.BlockSpec((1, gdn_head_dim), const))
        args.append(norm_g.reshape(1, gdn_head_dim))
    return pl.pallas_call(
        functools.partial(_mix_out_kernel, alpha=alpha, gdn_head_dim=gdn_head_dim),
        grid=(n // tm,),
        in_specs=in_specs,
        out_specs=[pl.BlockSpec((tm, d), row), pl.BlockSpec((8, tm), lambda i: (0, i))],
        out_shape=[jax.ShapeDtypeStruct((n, d), F32), jax.ShapeDtypeStruct((8, n), F32)],
        compiler_params=_cparams(("parallel",)),
        name="mix_out",
    )(*args)


def _moe_kernel(elo_ref, ehi_ref, nblk_ref, src_ref, src_next_ref, dst_ref, gate_ref, h_hbm,
                wg_lo_ref, wu_lo_ref, wd_lo_ref, wg_hi_ref, wu_hi_ref, wd_hi_ref,
                lng_ref, lnb_ref, out_hbm,
                xbuf, ybuf, wgu_c, wd_c, gsem, ssem, *, alpha, d_expert, n_tok):
    i = pl.program_id(0)
    nblk = nblk_ref[0]
    slot = lax.rem(i, 2)
    blk = xbuf.shape[1]
    prev = jnp.maximum(i - 1, 0)

    def gather(idx_ref, dst):
        for r in range(blk):
            pltpu.make_async_copy(h_hbm.at[pl.ds(idx_ref[0, 0, r], 1)],
                                  xbuf.at[dst, pl.ds(r, 1)], gsem.at[dst]).start()

    def wait_gather(dst):
        pltpu.make_async_copy(h_hbm.at[pl.ds(0, blk)], xbuf.at[dst], gsem.at[dst]).wait()

    def wait_scatter(src):
        pltpu.make_async_copy(ybuf.at[src], out_hbm.at[pl.ds(0, blk)], ssem.at[src]).wait()

    @pl.when(i == 0)
    def _():
        ybuf[...] = jnp.zeros(ybuf.shape, F32)
        spare = [pltpu.make_async_copy(ybuf.at[s], out_hbm.at[pl.ds(n_tok + s * blk, blk)],
                                       ssem.at[s]) for s in range(2)]
        for cp in spare:
            cp.start()
        for cp in spare:
            cp.wait()
        gather(src_ref, 0)

    @pl.when(i + 1 < nblk)
    def _():
        gather(src_next_ref, 1 - slot)

    def refresh(which, e_ref, wg_ref, wu_ref, wd_ref):
        @pl.when(jnp.logical_or(i == 0, e_ref[i] != e_ref[prev]))
        def _():
            wgu_c[which, :, :d_expert] = wg_ref[0, 0].astype(BF16)
            wgu_c[which, :, d_expert:] = wu_ref[0, 0].astype(BF16)
            wd_c[which] = wd_ref[0, 0].astype(BF16)

    refresh(0, elo_ref, wg_lo_ref, wu_lo_ref, wd_lo_ref)
    refresh(1, ehi_ref, wg_hi_ref, wu_hi_ref, wd_hi_ref)

    @pl.when(i < nblk)
    def _():
        wait_gather(slot)
        x = xbuf[slot]
        xb = x.astype(BF16)

        def ffn(which):
            a = jnp.dot(xb, wgu_c[which], preferred_element_type=F32)
            hid = _silu(a[:, :d_expert]) * a[:, d_expert:]
            return jnp.dot(hid.astype(BF16), wd_c[which], preferred_element_type=F32)

        y = ffn(0) * gate_ref[:, 0:1] + ffn(1) * gate_ref[:, 1:2]
        res = _layer_norm_rows(alpha * x + y, lng_ref[...], lnb_ref[...])

        @pl.when(i >= 2)
        def _():
            wait_scatter(slot)

        ybuf[slot] = res
        for r in range(blk):
            pltpu.make_async_copy(ybuf.at[slot, pl.ds(r, 1)],
                                  out_hbm.at[pl.ds(dst_ref[0, 0, r], 1)], ssem.at[slot]).start()

        @pl.when(i == nblk - 1)
        def _():
            wait_scatter(slot)

            @pl.when(i >= 1)
            def _():
                wait_scatter(1 - slot)


def _moe_ffn(h, n_tok, src_idx, dst_idx, gates_sorted, blk_lo, blk_hi, n_used, wg, wu, wd,
             layer, ln_g, ln_b, alpha):
    d = h.shape[1]
    d_expert = wd.shape[2]
    lo_blk = lambda i, lo, hi, nb: (layer, lo[i], 0, 0)
    hi_blk = lambda i, lo, hi, nb: (layer, hi[i], 0, 0)
    n_blocks = src_idx.shape[0] // MOE_BLOCK
    src = src_idx.reshape(n_blocks, 1, MOE_BLOCK)
    dst = dst_idx.reshape(n_blocks, 1, MOE_BLOCK)
    last = n_blocks - 1
    smem_blk = functools.partial(pl.BlockSpec, (1, 1, MOE_BLOCK), memory_space=pltpu.SMEM)
    grid_spec = pltpu.PrefetchScalarGridSpec(
        num_scalar_prefetch=3,
        grid=(n_blocks,),
        in_specs=[
            smem_blk(lambda i, lo, hi, nb: (i, 0, 0)),
            smem_blk(lambda i, lo, hi, nb: (jnp.minimum(i + 1, last), 0, 0)),
            smem_blk(lambda i, lo, hi, nb: (i, 0, 0)),
            pl.BlockSpec((MOE_BLOCK, 2), lambda i, lo, hi, nb: (i, 0)),
            pl.BlockSpec(memory_space=pl.ANY),
            pl.BlockSpec((1, 1, d, d_expert), lo_blk),
            pl.BlockSpec((1, 1, d, d_expert), lo_blk),
            pl.BlockSpec((1, 1, d_expert, d), lo_blk),
            pl.BlockSpec((1, 1, d, d_expert), hi_blk),
            pl.BlockSpec((1, 1, d, d_expert), hi_blk),
            pl.BlockSpec((1, 1, d_expert, d), hi_blk),
            pl.BlockSpec((1, d), lambda i, lo, hi, nb: (0, 0)),
            pl.BlockSpec((1, d), lambda i, lo, hi, nb: (0, 0)),
        ],
        out_specs=pl.BlockSpec(memory_space=pl.ANY),
        scratch_shapes=[
            pltpu.VMEM((2, MOE_BLOCK, d), F32),
            pltpu.VMEM((2, MOE_BLOCK, d), F32),
            pltpu.VMEM((2, d, 2 * d_expert), BF16),
            pltpu.VMEM((2, d_expert, d), BF16),
            pltpu.SemaphoreType.DMA((2,)),
            pltpu.SemaphoreType.DMA((2,)),
        ],
    )
    return pl.pallas_call(
        functools.partial(_moe_kernel, alpha=alpha, d_expert=d_expert, n_tok=n_tok),
        grid_spec=grid_spec,
        out_shape=jax.ShapeDtypeStruct((n_tok + MOE_SPARE_ROWS, d), F32),
        compiler_params=_cparams(("arbitrary",)),
        name="moe_ffn",
    )(blk_lo, blk_hi, n_used, src, src, dst, gates_sorted, h, wg, wu, wd, wg, wu, wd,
      ln_g.reshape(1, d), ln_b.reshape(1, d))


def _dispatch_plan(route, n_tok):
    e1 = route[0].astype(jnp.int32)
    e2 = route[1].astype(jnp.int32)
    first_is_lo = e1 < e2
    lo = jnp.where(first_is_lo, e1, e2)
    hi = jnp.where(first_is_lo, e2, e1)
    g_lo = jnp.where(first_is_lo, route[2], route[3])
    g_hi = jnp.where(first_is_lo, route[3], route[2])
    grp = lo // EXPERTS_PER_GROUP
    a = lo % EXPERTS_PER_GROUP
    b = hi % EXPERTS_PER_GROUP
    cls = grp * PAIRS_PER_GROUP + (a * (7 - a)) // 2 + (b - a - 1)
    n_cls = N_GROUPS * PAIRS_PER_GROUP
    n_blocks = -(-(n_tok + n_cls * (MOE_BLOCK - 1)) // MOE_BLOCK)
    n_slots = n_blocks * MOE_BLOCK

    counts = jnp.bincount(cls, length=n_cls)
    start = jnp.cumsum(counts) - counts
    padded = (counts + MOE_BLOCK - 1) // MOE_BLOCK * MOE_BLOCK
    pend = jnp.cumsum(padded)
    pstart = pend - padded
    order = jnp.argsort(cls, stable=True).astype(jnp.int32)
    slot_ids = jnp.arange(n_slots)
    slot_cls = jnp.minimum(jnp.sum(slot_ids[:, None] >= pend[None, :], axis=1), n_cls - 1)
    rank = slot_ids - pstart[slot_cls]
    valid = rank < counts[slot_cls]
    slot_tok = jnp.where(valid, order[jnp.clip(start[slot_cls] + rank, 0, n_tok - 1)], 0)
    slot_tok = slot_tok.astype(jnp.int32)
    spare = n_tok + (slot_ids // MOE_BLOCK % 2) * MOE_BLOCK + slot_ids % MOE_BLOCK
    slot_dst = jnp.where(valid, slot_tok, spare).astype(jnp.int32)
    blk_cls = slot_cls[::MOE_BLOCK]
    pair_lo = jnp.array([0, 0, 0, 1, 1, 2], jnp.int32)
    pair_hi = jnp.array([1, 2, 3, 2, 3, 3], jnp.int32)
    blk_grp = blk_cls // PAIRS_PER_GROUP
    blk_pair = blk_cls % PAIRS_PER_GROUP
    blk_lo = (blk_grp * EXPERTS_PER_GROUP + pair_lo[blk_pair]).astype(jnp.int32)
    blk_hi = (blk_grp * EXPERTS_PER_GROUP + pair_hi[blk_pair]).astype(jnp.int32)
    n_used = (pend[-1] // MOE_BLOCK).astype(jnp.int32).reshape(1)
    gates_sorted = jnp.stack([g_lo[slot_tok], g_hi[slot_tok]], axis=1)
    return slot_tok, slot_dst, gates_sorted, blk_lo, blk_hi, n_used


def _moe_layer(h, route, wg, wu, wd, layer, ln_g, ln_b, alpha):
    n_tok = h.shape[0]
    slot_tok, slot_dst, gates_sorted, blk_lo, blk_hi, n_used = _dispatch_plan(route, n_tok)
    return _moe_ffn(h, n_tok, slot_tok, slot_dst, gates_sorted, blk_lo, blk_hi, n_used,
                    wg, wu, wd, layer, ln_g, ln_b, alpha)


def _pad_cols(w, width):
    return jnp.pad(w, ((0, 0), (0, width - w.shape[1])))


def kernel(x, ln_mix_g, ln_mix_b, ln_ffn_g, ln_ffn_b, router_w, router_b, fox_w_in, fox_b_f,
           fox_q_gain, fox_k_gain, fox_w_out, gdn_w_in, gdn_conv_w, gdn_a_log, gdn_dt_bias,
           gdn_norm_g, gdn_w_out, moe_w_gate, moe_w_up, moe_w_down):
    bsz, seq, d = x.shape
    n = bsz * seq
    depth = ln_mix_g.shape[0]
    alpha = (2.0 * depth) ** 0.25
    fox_heads = fox_b_f.shape[1]
    fox_dim = d // fox_heads
    gdn_heads = gdn_a_log.shape[1]
    gdn_dim = d // gdn_heads
    rw_t = router_w.T

    h = x.reshape(n, d)
    for i in range(depth):
        j = i // 2
        if i % 2 == 0:
            w_in = fox_w_in[j]
            gains = jnp.concatenate([jnp.tile(fox_q_gain[j], fox_heads) * (fox_dim ** -0.5 * LOG2E),
                                     jnp.tile(fox_k_gain[j], fox_heads)]).reshape(1, 2 * d)
            qk3, v_t, og, side = _inproj_fox(h, bsz, seq, w_in[:, :4 * d].astype(BF16),
                                             _pad_cols(w_in[:, 4 * d:], LANES).astype(BF16),
                                             gains, fox_dim)
            f_t = side[:, :fox_heads].reshape(bsz, seq, fox_heads).transpose(0, 2, 1)
            c = _fox_gates(f_t, fox_b_f[j])
            c_row = c.reshape(bsz, fox_heads // 2, 2, seq)
            o = _fox_attention(qk3, v_t, c_row, fox_heads, fox_dim)
            h, route = _mix_out(o.reshape(n, d), og, h, fox_w_out[j].astype(BF16),
                                ln_mix_g[i], ln_mix_b[i], rw_t, router_b, alpha)
        else:
            w_in = gdn_w_in[j]
            qkv, og, side = _inproj_gdn(h, bsz, seq, w_in[:, :4 * d].astype(BF16),
                                        _pad_cols(w_in[:, 4 * d:], LANES).astype(BF16),
                                        gdn_conv_w[j], gdn_dim)
            side3 = side.reshape(bsz, seq, LANES)
            b_t = side3[:, :, :gdn_heads].transpose(0, 2, 1)
            a_t = side3[:, :, gdn_heads:2 * gdn_heads].transpose(0, 2, 1)
            beta, gc = _gdn_gates(b_t, a_t, gdn_a_log[j], gdn_dt_bias[j])
            rows = jnp.concatenate([beta, gc], axis=1).reshape(
                bsz, 2 * gdn_heads, seq // CHUNK, CHUNK).transpose(0, 2, 1, 3)
            cols = rows.transpose(0, 1, 3, 2)
            o = _gdn_delta_rule(qkv, cols, rows, gdn_heads, gdn_dim)
            h, route = _mix_out(o.reshape(n, d), og, h, gdn_w_out[j].astype(BF16),
                                ln_mix_g[i], ln_mix_b[i], rw_t, router_b, alpha,
                                norm_g=gdn_norm_g[j], gdn_head_dim=gdn_dim)
        h = _moe_layer(h, route, moe_w_gate, moe_w_up, moe_w_down, i,
                       ln_ffn_g[i], ln_ffn_b[i], alpha)
    return h[:n].reshape(bsz, seq, d)
```

```python
import functools
import math

import jax
import jax.numpy as jnp
from jax import lax
from jax.experimental import pallas as pl
from jax.experimental.pallas import tpu as pltpu

F32 = jnp.float32
BF16 = jnp.bfloat16

N_GROUPS = 4
EXPERTS_PER_GROUP = 4
PAIRS_PER_GROUP = 6
CHUNK = 64
CONV_WIDTH = 4
LN_EPS = 1e-5
RMS_EPS = 1e-6
L2_EPS = 1e-6

LOG2E = math.log2(math.e)

LANES = 128
BF16_ROWS = 16
VMEM_LIMIT = 56 * 1024 * 1024

ROW_TILE = 512
ATT_Q_TILE = 512
GDN_TILE = 256
GDN_UNROLL = 2
MOE_BLOCK = 256
MOE_SPARE_ROWS = 2 * MOE_BLOCK


def _cparams(sem):
    return pltpu.CompilerParams(dimension_semantics=sem, vmem_limit_bytes=VMEM_LIMIT)


def _layer_norm_rows(y, g, b):
    mu = jnp.mean(y, axis=-1, keepdims=True)
    yc = y - mu
    var = jnp.mean(yc * yc, axis=-1, keepdims=True)
    return yc * lax.rsqrt(var + LN_EPS) * g + b


def _sigmoid(x):
    return 1.0 / (1.0 + jnp.exp(-x))


def _silu(x):
    return x * _sigmoid(x)


def _softplus(x):
    return jnp.maximum(x, 0.0) + jnp.log(1.0 + jnp.exp(-jnp.abs(x)))


def _inproj_fox_kernel(x_ref, w_ref, ws_ref, gain_ref, qk_ref, vt_ref, og_ref, os_ref,
                       *, head_dim):
    xb = x_ref[...].astype(BF16)
    d = og_ref.shape[1]
    lane = lax.broadcasted_iota(jnp.int32, (1, LANES), 1)
    first = lane < head_dim

    def part(p):
        return jnp.dot(xb, w_ref[:, p * d:(p + 1) * d], preferred_element_type=F32)

    for p in range(2):
        acc = part(p)
        for j in range(d // LANES):
            x = acc[:, j * LANES:(j + 1) * LANES]
            xx = x * x
            s_a = jnp.sum(jnp.where(first, xx, 0.0), axis=-1, keepdims=True)
            s_b = jnp.sum(jnp.where(first, 0.0, xx), axis=-1, keepdims=True)
            r = jnp.where(first, lax.rsqrt(s_a / head_dim + RMS_EPS),
                          lax.rsqrt(s_b / head_dim + RMS_EPS))
            sl = slice(p * d + j * LANES, p * d + (j + 1) * LANES)
            qk_ref[0, :, sl] = (x * r * gain_ref[:, sl]).astype(BF16)
    acc = part(2)
    for j in range(d // LANES):
        vt_ref[0, j * LANES:(j + 1) * LANES, :] = acc[:, j * LANES:(j + 1) * LANES].T.astype(BF16)
    og_ref[...] = part(3).astype(BF16)
    os_ref[...] = jnp.dot(xb, ws_ref[...], preferred_element_type=F32)


def _inproj_fox(x, bsz, seq, w_main, w_side, gains, head_dim):
    d = x.shape[1]
    n = bsz * seq
    ts = min(ROW_TILE, seq)
    tiles = seq // ts
    row = lambda b, i: (b * tiles + i, 0)
    const = lambda b, i: (0, 0)
    return pl.pallas_call(
        functools.partial(_inproj_fox_kernel, head_dim=head_dim),
        grid=(bsz, tiles),
        in_specs=[
            pl.BlockSpec((ts, d), row),
            pl.BlockSpec((d, 4 * d), const),
            pl.BlockSpec((d, LANES), const),
            pl.BlockSpec((1, 2 * d), const),
        ],
        out_specs=[
            pl.BlockSpec((1, ts, 2 * d), lambda b, i: (b, i, 0)),
            pl.BlockSpec((1, d, ts), lambda b, i: (b, 0, i)),
            pl.BlockSpec((ts, d), row),
            pl.BlockSpec((ts, LANES), row),
        ],
        out_shape=[
            jax.ShapeDtypeStruct((bsz, seq, 2 * d), BF16),
            jax.ShapeDtypeStruct((bsz, d, seq), BF16),
            jax.ShapeDtypeStruct((n, d), BF16),
            jax.ShapeDtypeStruct((n, LANES), F32),
        ],
        compiler_params=_cparams(("parallel", "parallel")),
        name="inproj_fox",
    )(x, w_main, w_side, gains)


def _inproj_gdn_kernel(x_ref, w_ref, ws_ref, cw_ref, qkv_ref, og_ref, os_ref, buf_ref,
                       *, q_scale):
    ts = x_ref.shape[0]
    d = og_ref.shape[1]
    pad = 8

    @pl.when(pl.program_id(1) == 0)
    def _():
        buf_ref[0:pad, :] = jnp.zeros((pad, buf_ref.shape[1]), F32)

    xb = x_ref[...].astype(BF16)

    def project(p):
        buf_ref[pad:pad + ts, p * d:(p + 1) * d] = jnp.dot(
            xb, w_ref[:, p * d:(p + 1) * d], preferred_element_type=F32)

    def finish(p):
        for j in range(p * d // LANES, (p + 1) * d // LANES):
            sl = slice(j * LANES, (j + 1) * LANES)
            acc = buf_ref[pad:pad + ts, sl] * cw_ref[CONV_WIDTH - 1:CONV_WIDTH, sl]
            for dlt in range(1, CONV_WIDTH):
                acc = acc + (buf_ref[pad - dlt:pad - dlt + ts, sl]
                             * cw_ref[CONV_WIDTH - 1 - dlt:CONV_WIDTH - dlt, sl])
            y = _silu(acc)
            if p < 2:
                y = y * lax.rsqrt(jnp.sum(y * y, axis=-1, keepdims=True) + L2_EPS)
                if p == 0:
                    y = y * q_scale
            qkv_ref[0, :, sl] = y.astype(BF16)

    project(0)
    project(1)
    finish(0)
    project(2)
    finish(1)
    og_ref[...] = jnp.dot(xb, w_ref[:, 3 * d:4 * d], preferred_element_type=F32).astype(BF16)
    finish(2)
    os_ref[...] = jnp.dot(xb, ws_ref[...], preferred_element_type=F32)
    buf_ref[0:pad, :] = buf_ref[ts:ts + pad, :]


def _inproj_gdn(x, bsz, seq, w_main, w_side, conv_w, head_dim):
    d = x.shape[1]
    n = bsz * seq
    ts = min(ROW_TILE, seq)
    tiles = seq // ts
    row = lambda b, i: (b * tiles + i, 0)
    const = lambda b, i: (0, 0)
    return pl.pallas_call(
        functools.partial(_inproj_gdn_kernel, q_scale=head_dim ** -0.5),
        grid=(bsz, tiles),
        in_specs=[
            pl.BlockSpec((ts, d), row),
            pl.BlockSpec((d, 4 * d), const),
            pl.BlockSpec((d, LANES), const),
            pl.BlockSpec((CONV_WIDTH, 3 * d), const),
        ],
        out_specs=[
            pl.BlockSpec((1, ts, 3 * d), lambda b, i: (b, i, 0)),
            pl.BlockSpec((ts, d), row),
            pl.BlockSpec((ts, LANES), row),
        ],
        out_shape=[
            jax.ShapeDtypeStruct((bsz, seq, 3 * d), BF16),
            jax.ShapeDtypeStruct((n, d), BF16),
            jax.ShapeDtypeStruct((n, LANES), F32),
        ],
        scratch_shapes=[pltpu.VMEM((ts + 8, 3 * d), F32)],
        compiler_params=_cparams(("parallel", "arbitrary")),
        name="inproj_gdn",
    )(x, w_main, w_side, conv_w)


def _prefix_sum_lanes(x, period):
    pos = lax.broadcasted_iota(jnp.int32, x.shape, 1) % period
    d = 1
    while d < period:
        x = x + jnp.where(pos >= d, pltpu.roll(x, d, axis=1), 0.0)
        d *= 2
    return x


def _fox_gate_kernel(f_ref, b_ref, c_ref):
    z = f_ref[0] + b_ref[...]
    log_f = -_softplus(-z)
    c_ref[0] = _prefix_sum_lanes(log_f, z.shape[1])


def _fox_gates(f_logit_t, b_f):
    bsz, h, s = f_logit_t.shape
    return pl.pallas_call(
        _fox_gate_kernel,
        grid=(bsz,),
        in_specs=[pl.BlockSpec((1, h, s), lambda b: (b, 0, 0)),
                  pl.BlockSpec((h, 1), lambda b: (0, 0))],
        out_specs=pl.BlockSpec((1, h, s), lambda b: (b, 0, 0)),
        out_shape=jax.ShapeDtypeStruct((bsz, h, s), F32),
        compiler_params=_cparams(("parallel",)),
        name="fox_gates",
    )(f_logit_t, b_f.reshape(h, 1))


def _gdn_gate_kernel(b_ref, a_ref, alog_ref, dt_ref, beta_ref, gc_ref):
    beta_ref[0] = _sigmoid(b_ref[0])
    g = -jnp.exp(alog_ref[...]) * _softplus(a_ref[0] + dt_ref[...])
    gc_ref[0] = _prefix_sum_lanes(g, CHUNK)


def _gdn_gates(b_logit_t, a_logit_t, a_log, dt_bias):
    bsz, h, s = b_logit_t.shape
    spec = pl.BlockSpec((1, h, s), lambda b: (b, 0, 0))
    vec = pl.BlockSpec((h, 1), lambda b: (0, 0))
    return pl.pallas_call(
        _gdn_gate_kernel,
        grid=(bsz,),
        in_specs=[spec, spec, vec, vec],
        out_specs=[spec, spec],
        out_shape=[jax.ShapeDtypeStruct((bsz, h, s), F32)] * 2,
        compiler_params=_cparams(("parallel",)),
        name="gdn_gates",
    )(b_logit_t, a_logit_t, a_log.reshape(h, 1), dt_bias.reshape(h, 1))


def _fox_attn_kernel(q_ref, k_ref, vt_ref, crow_ref, o_ref, ckb_ref, acc_ref, sbuf_ref,
                     *, head_dim, tq, tk):
    qi = pl.program_id(2)
    seq = k_ref.shape[1]

    @pl.when(qi == 0)
    def _():
        def fill(j, carry):
            st = pl.multiple_of(j * LANES, LANES)
            rows = crow_ref[0, 0, :, pl.ds(st, LANES)] * LOG2E
            for hh in range(2):
                ckb_ref[hh, pl.ds(st, LANES), :] = jnp.broadcast_to(
                    rows[hh:hh + 1, :], (LANES, LANES)).T
            return carry
        lax.fori_loop(0, seq // LANES, fill, 0)

    lane = lax.broadcasted_iota(jnp.int32, (1, LANES), 1)
    first = lane < head_dim
    q = q_ref[0]
    zero = jnp.zeros_like(q)
    q_heads = (jnp.where(first, q, zero), jnp.where(first, zero, q))
    q_start = pl.multiple_of(qi * tq, tq)
    c_q = crow_ref[0, 0, :, pl.ds(q_start, tq)] * LOG2E
    acc_ref[...] = jnp.zeros(acc_ref.shape, F32)
    ones = jnp.ones((acc_ref.shape[1] - head_dim, tk), BF16)

    def scores(j):
        start = pl.multiple_of(j * tk, tk)
        k = k_ref[0, pl.ds(start, tk), :]
        return [lax.dot_general(k, q_heads[hh], (((1,), (1,)), ((), ())),
                                preferred_element_type=F32) for hh in range(2)]

    def update(j, s_in, stats, masked):
        start = pl.multiple_of(j * tk, tk)
        out = []
        for hh in range(2):
            m_old = stats[hh]
            s0 = s_in[hh]
            cb = ckb_ref[hh, pl.ds(start, tk), :]
            slabs = []
            for t in range(tq // LANES):
                sl = s0[:, t * LANES:(t + 1) * LANES] - cb
                if masked:
                    kpos = start + lax.broadcasted_iota(jnp.int32, sl.shape, 0)
                    qpos = q_start + t * LANES + lax.broadcasted_iota(jnp.int32, sl.shape, 1)
                    sl = jnp.where(kpos <= qpos, sl, -jnp.inf)
                slabs.append(sl)
            s = jnp.concatenate(slabs, axis=1)
            cq = c_q[hh:hh + 1, :]
            m_new = jnp.maximum(m_old, jnp.max(s, axis=0, keepdims=True) + cq)
            p = jnp.exp2(s + (cq - m_new))
            alpha = jnp.exp2(m_old - m_new)
            vt = jnp.concatenate(
                [vt_ref[0, hh * head_dim:(hh + 1) * head_dim, pl.ds(start, tk)], ones], axis=0)
            pv = jnp.dot(vt, p.astype(BF16), preferred_element_type=F32)
            acc_ref[hh] = acc_ref[hh] * alpha + pv
            out.append(m_new)
        return tuple(out)

    def park(s):
        sbuf_ref[0] = s[0]
        sbuf_ref[1] = s[1]

    def parked():
        return [sbuf_ref[0], sbuf_ref[1]]

    def run(j0, n, stats):
        cur = parked()
        for t in range(n):
            nxt = scores(j0 + t + 1)
            if t == n - 1:
                park(nxt)
            stats = update(j0 + t, cur, stats, False)
            cur = nxt
        return stats

    neg = jnp.full((1, tq), -1e30, F32)
    park(scores(0))
    quads = qi // 2
    stats = lax.fori_loop(0, quads, lambda i, st: run(4 * i, 4, st), (neg, neg))
    stats = lax.fori_loop(2 * quads, qi, lambda i, st: run(2 * i, 2, st), stats)
    j0 = 2 * qi
    s1 = scores(j0 + 1)
    stats = update(j0, parked(), stats, True)
    update(j0 + 1, s1, stats, True)
    o_t = jnp.concatenate(
        [acc_ref[hh, :head_dim, :] / acc_ref[hh, head_dim:head_dim + 1, :] for hh in range(2)],
        axis=0)
    o_ref[0] = o_t.T.astype(BF16)


def _fox_attention(qk3, v_t, c_row, n_heads, head_dim):
    bsz, seq, _ = qk3.shape
    d = n_heads * head_dim
    pairs = d // LANES
    tq = min(ATT_Q_TILE, seq)
    tk = tq // 2
    assert seq % tq == 0
    return pl.pallas_call(
        functools.partial(_fox_attn_kernel, head_dim=head_dim, tq=tq, tk=tk),
        grid=(bsz, pairs, seq // tq),
        in_specs=[
            pl.BlockSpec((1, tq, LANES), lambda b, p, i: (b, i, p)),
            pl.BlockSpec((1, seq, LANES), lambda b, p, i: (b, 0, pairs + p)),
            pl.BlockSpec((1, LANES, seq), lambda b, p, i: (b, p, 0)),
            pl.BlockSpec((1, 1, 2, seq), lambda b, p, i: (b, p, 0, 0)),
        ],
        out_specs=pl.BlockSpec((1, tq, LANES), lambda b, p, i: (b, i, p)),
        out_shape=jax.ShapeDtypeStruct((bsz, seq, d), BF16),
        scratch_shapes=[
            pltpu.VMEM((2, seq, LANES), F32),
            pltpu.VMEM((2, head_dim + BF16_ROWS, tq), F32),
            pltpu.VMEM((2, tk, tq), F32),
        ],
        compiler_params=_cparams(("parallel", "parallel", "arbitrary")),
        name="fox_attention",
    )(qk3, qk3, v_t, c_row)


def _gdn_chunk_kernel(q_ref, k_ref, v_ref, col_ref, row_ref, o_ref, state_ref,
                      *, n_heads, head_dim):
    @pl.when(pl.program_id(1) == 0)
    def _():
        state_ref[...] = jnp.zeros(state_ref.shape, F32)

    n_chunks = q_ref.shape[1] // CHUNK
    ri = lax.broadcasted_iota(jnp.int32, (CHUNK, CHUNK), 0)
    ci = lax.broadcasted_iota(jnp.int32, (CHUNK, CHUNK), 1)
    eye = jnp.where(ri == ci, 1.0, 0.0)
    heads = range(n_heads)
    nt = (((1,), (1,)), ((), ()))

    def mm(a, b):
        return jnp.dot(a, b, preferred_element_type=F32)

    def stack(a, b):
        return jnp.concatenate([a, b], axis=0)

    def chunk(it, carry):
        first = it * GDN_UNROLL
        starts = [pl.multiple_of((first + cc) * CHUNK, CHUNK) for cc in range(GDN_UNROLL)]
        cols = [col_ref[0, first + cc] for cc in range(GDN_UNROLL)]
        rows = [row_ref[0, first + cc] for cc in range(GDN_UNROLL)]
        probs = range(GDN_UNROLL * n_heads)
        cc_of = [p // n_heads for p in probs]
        h_of = [p % n_heads for p in probs]
        sls = [slice(h * head_dim, (h + 1) * head_dim) for h in heads]
        q = [q_ref[0, pl.ds(starts[cc_of[p]], CHUNK), sls[h_of[p]]] for p in probs]
        k = [k_ref[0, pl.ds(starts[cc_of[p]], CHUNK), sls[h_of[p]]] for p in probs]
        v = [v_ref[0, pl.ds(starts[cc_of[p]], CHUNK), sls[h_of[p]]] for p in probs]
        beta = [cols[cc_of[p]][:, h_of[p]:h_of[p] + 1] for p in probs]
        g_col = [cols[cc_of[p]][:, n_heads + h_of[p]:n_heads + h_of[p] + 1] for p in probs]
        g_row = [rows[cc_of[p]][n_heads + h_of[p]:n_heads + h_of[p] + 1, :] for p in probs]
        g_last = [g_col[p][CHUNK - 1:CHUNK, :] for p in probs]
        decay = [jnp.exp(jnp.where(ri >= ci, g_col[p] - g_row[p], -jnp.inf)) for p in probs]
        kf = [k[p].astype(F32) for p in probs]
        kb = [kf[p] * beta[p] for p in probs]

        r1 = [lax.dot_general(stack(kb[p].astype(BF16), q[p]), k[p], nt,
                              preferred_element_type=F32) for p in probs]
        m = [jnp.where(ri > ci, r1[p][:CHUNK] * decay[p], 0.0) for p in probs]
        attn = [(r1[p][CHUNK:] * decay[p]).astype(BF16) for p in probs]

        t = [eye - m[p] for p in probs]
        pb = [m[p].astype(BF16) for p in probs]
        pw = [mm(pb[p], pb[p]) for p in probs]
        power = 2
        while power < CHUNK:
            pb = [pw[p].astype(BF16) for p in probs]
            if 2 * power >= CHUNK:
                t = [t[p] + mm(t[p].astype(BF16), pb[p]) for p in probs]
            else:
                r = [mm(stack(t[p].astype(BF16), pb[p]), pb[p]) for p in probs]
                t = [t[p] + r[p][:CHUNK] for p in probs]
                pw = [r[p][CHUNK:] for p in probs]
            power *= 2

        e_g = [jnp.exp(g_col[p]) for p in probs]
        rhs = [jnp.concatenate([v[p].astype(F32) * beta[p], kb[p] * e_g[p]], axis=1).astype(BF16)
               for p in probs]
        sol = [mm(t[p].astype(BF16), rhs[p]) for p in probs]
        q_dec = [(q[p].astype(F32) * e_g[p]).astype(BF16) for p in probs]
        k_dec_t = [(kf[p] * jnp.exp(g_last[p] - g_col[p])).T.astype(BF16) for p in probs]

        for cc in range(GDN_UNROLL):
            ps = [cc * n_heads + h for h in heads]
            sb = [state_ref[h].astype(BF16) for h in heads]
            r2 = [mm(stack(sol[ps[h]][:, head_dim:].astype(BF16), q_dec[ps[h]]), sb[h])
                  for h in heads]
            vb = [(sol[ps[h]][:, :head_dim] - r2[h][:CHUNK]).astype(BF16) for h in heads]
            r3 = [mm(stack(attn[ps[h]], k_dec_t[ps[h]]), vb[h]) for h in heads]
            for h in heads:
                o_ref[0, pl.ds(starts[cc], CHUNK), sls[h]] = (
                    r2[h][CHUNK:] + r3[h][:CHUNK]).astype(BF16)
                state_ref[h] = state_ref[h] * jnp.exp(g_last[ps[h]]) + r3[h][CHUNK:]
        return carry

    lax.fori_loop(0, n_chunks // GDN_UNROLL, chunk, 0)


def _gdn_delta_rule(qkv3, cols, rows, n_heads, head_dim):
    bsz, seq, _ = qkv3.shape
    d = n_heads * head_dim
    ts = min(GDN_TILE, seq)
    cpt = ts // CHUNK
    return pl.pallas_call(
        functools.partial(_gdn_chunk_kernel, n_heads=n_heads, head_dim=head_dim),
        grid=(bsz, seq // ts),
        in_specs=[
            pl.BlockSpec((1, ts, d), lambda b, i: (b, i, 0)),
            pl.BlockSpec((1, ts, d), lambda b, i: (b, i, 1)),
            pl.BlockSpec((1, ts, d), lambda b, i: (b, i, 2)),
            pl.BlockSpec((1, cpt, CHUNK, 2 * n_heads), lambda b, i: (b, i, 0, 0)),
            pl.BlockSpec((1, cpt, 2 * n_heads, CHUNK), lambda b, i: (b, i, 0, 0)),
        ],
        out_specs=pl.BlockSpec((1, ts, d), lambda b, i: (b, i, 0)),
        out_shape=jax.ShapeDtypeStruct((bsz, seq, d), BF16),
        scratch_shapes=[pltpu.VMEM((n_heads, head_dim, head_dim), F32)],
        compiler_params=_cparams(("parallel", "arbitrary")),
        name="gdn_delta_rule",
    )(qkv3, qkv3, qkv3, cols, rows)


def _route_rows(logits_t, bias):
    n_exp = logits_t.shape[0]
    mx = jnp.max(logits_t, axis=0, keepdims=True)
    ex = jnp.exp(logits_t - mx)
    probs = ex / jnp.sum(ex, axis=0, keepdims=True)
    sel = probs + bias
    sel_r = [sel[e:e + 1, :] for e in range(n_exp)]
    prob_r = [probs[e:e + 1, :] for e in range(n_exp)]

    def top2_sum(a, b, c, d):
        hi1, lo1 = jnp.maximum(a, b), jnp.minimum(a, b)
        hi2, lo2 = jnp.maximum(c, d), jnp.minimum(c, d)
        return jnp.maximum(hi1, hi2) + jnp.maximum(jnp.minimum(hi1, hi2), jnp.maximum(lo1, lo2))

    scores = [top2_sum(*sel_r[EXPERTS_PER_GROUP * g:EXPERTS_PER_GROUP * (g + 1)])
              for g in range(N_GROUPS)]
    best = scores[0]
    gidx = jnp.zeros(best.shape, jnp.int32)
    for g in range(1, N_GROUPS):
        upd = scores[g] > best
        best = jnp.where(upd, scores[g], best)
        gidx = jnp.where(upd, g, gidx)

    def pick(vals, j):
        out = vals[j]
        for g in range(1, N_GROUPS):
            out = jnp.where(gidx == g, vals[EXPERTS_PER_GROUP * g + j], out)
        return out

    cand = [pick(sel_r, j) for j in range(EXPERTS_PER_GROUP)]
    cprob = [pick(prob_r, j) for j in range(EXPERTS_PER_GROUP)]
    b1, i1, p1 = cand[0], jnp.zeros(best.shape, jnp.int32), cprob[0]
    for j in range(1, EXPERTS_PER_GROUP):
        upd = cand[j] > b1
        b1 = jnp.where(upd, cand[j], b1)
        i1 = jnp.where(upd, j, i1)
        p1 = jnp.where(upd, cprob[j], p1)
    b2 = jnp.full(best.shape, -jnp.inf, F32)
    i2 = jnp.full(best.shape, -1, jnp.int32)
    p2 = jnp.zeros(best.shape, F32)
    for j in range(EXPERTS_PER_GROUP):
        upd = jnp.logical_and(i1 != j, jnp.logical_or(cand[j] > b2, i2 < 0))
        b2 = jnp.where(upd, cand[j], b2)
        i2 = jnp.where(upd, j, i2)
        p2 = jnp.where(upd, cprob[j], p2)
    tot = p1 + p2
    base = gidx * EXPERTS_PER_GROUP
    return base + i1, base + i2, p1 / tot, p2 / tot


def _mix_out_kernel(o_ref, og_ref, h_ref, w_ref, lng_ref, lnb_ref, rw_ref, rb_ref,
                    *rest, alpha, gdn_head_dim):
    if gdn_head_dim:
        ng_ref, hn_ref, route_ref = rest
    else:
        hn_ref, route_ref = rest
    o = o_ref[...].astype(F32)
    og = og_ref[...].astype(F32)
    if gdn_head_dim:
        parts = []
        for j in range(o.shape[1] // gdn_head_dim):
            x = o[:, j * gdn_head_dim:(j + 1) * gdn_head_dim]
            r = lax.rsqrt(jnp.mean(x * x, axis=-1, keepdims=True) + RMS_EPS)
            parts.append(x * r * ng_ref[...])
        a = jnp.concatenate(parts, axis=1) * _silu(og)
    else:
        a = o * _sigmoid(og)
    mix = jnp.dot(a.astype(BF16), w_ref[...], preferred_element_type=F32)
    hn = _layer_norm_rows(alpha * h_ref[...] + mix, lng_ref[...], lnb_ref[...])
    hn_ref[...] = hn
    logits_t = lax.dot_general(rw_ref[...], hn, (((1,), (1,)), ((), ())),
                               preferred_element_type=F32,
                               precision=lax.Precision.HIGHEST)
    e1, e2, g1, g2 = _route_rows(logits_t, rb_ref[...])
    route_ref[0:1, :] = e1.astype(F32)
    route_ref[1:2, :] = e2.astype(F32)
    route_ref[2:3, :] = g1
    route_ref[3:4, :] = g2
    route_ref[4:8, :] = jnp.zeros((4, g1.shape[1]), F32)


def _mix_out(o, og, h, w_out, ln_g, ln_b, rw_t, rb, alpha,
             norm_g=None, gdn_head_dim=0):
    n, d = o.shape
    n_exp = rw_t.shape[0]
    tm = min(ROW_TILE, n)
    row = lambda i: (i, 0)
    const = lambda i: (0, 0)
    in_specs = [
        pl.BlockSpec((tm, d), row),
        pl.BlockSpec((tm, d), row),
        pl.BlockSpec((tm, d), row),
        pl.BlockSpec((d, d), const),
        pl.BlockSpec((1, d), const),
        pl.BlockSpec((1, d), const),
        pl.BlockSpec((n_exp, d), const),
        pl.BlockSpec((n_exp, 1), const),
    ]
    args = [o, og, h, w_out, ln_g.reshape(1, d), ln_b.reshape(1, d), rw_t,
            rb.reshape(n_exp, 1)]
    if gdn_head_dim:
        in_specs.append(pl.BlockSpec((1, gdn_head_dim), const))
        args.append(norm_g.reshape(1, gdn_head_dim))
    return pl.pallas_call(
        functools.partial(_mix_out_kernel, alpha=alpha, gdn_head_dim=gdn_head_dim),
        grid=(n // tm,),
        in_specs=in_specs,
        out_specs=[pl.BlockSpec((tm, d), row), pl.BlockSpec((8, tm), lambda i: (0, i))],
        out_shape=[jax.ShapeDtypeStruct((n, d), F32), jax.ShapeDtypeStruct((8, n), F32)],
        compiler_params=_cparams(("parallel",)),
        name="mix_out",
    )(*args)


def _moe_kernel(elo_ref, ehi_ref, nblk_ref, src_ref, src_next_ref, dst_ref, gate_ref, h_hbm,
                wg_lo_ref, wu_lo_ref, wd_lo_ref, wg_hi_ref, wu_hi_ref, wd_hi_ref,
                lng_ref, lnb_ref, out_hbm,
                xbuf, ybuf, wgu_c, wd_c, gsem, ssem, *, alpha, d_expert, n_tok):
    i = pl.program_id(0)
    nblk = nblk_ref[0]
    slot = lax.rem(i, 2)
    blk = xbuf.shape[1]
    prev = jnp.maximum(i - 1, 0)

    def gather(idx_ref, dst):
        for r in range(blk):
            pltpu.make_async_copy(h_hbm.at[pl.ds(idx_ref[0, 0, r], 1)],
                                  xbuf.at[dst, pl.ds(r, 1)], gsem.at[dst]).start()

    def wait_gather(dst):
        pltpu.make_async_copy(h_hbm.at[pl.ds(0, blk)], xbuf.at[dst], gsem.at[dst]).wait()

    def wait_scatter(src):
        pltpu.make_async_copy(ybuf.at[src], out_hbm.at[pl.ds(0, blk)], ssem.at[src]).wait()

    @pl.when(i == 0)
    def _():
        ybuf[...] = jnp.zeros(ybuf.shape, F32)
        spare = [pltpu.make_async_copy(ybuf.at[s], out_hbm.at[pl.ds(n_tok + s * blk, blk)],
                                       ssem.at[s]) for s in range(2)]
        for cp in spare:
            cp.start()
        for cp in spare:
            cp.wait()
        gather(src_ref, 0)

    @pl.when(i + 1 < nblk)
    def _():
        gather(src_next_ref, 1 - slot)

    def refresh(which, e_ref, wg_ref, wu_ref, wd_ref):
        @pl.when(jnp.logical_or(i == 0, e_ref[i] != e_ref[prev]))
        def _():
            wgu_c[which, :, :d_expert] = wg_ref[0, 0].astype(BF16)
            wgu_c[which, :, d_expert:] = wu_ref[0, 0].astype(BF16)
            wd_c[which] = wd_ref[0, 0].astype(BF16)

    refresh(0, elo_ref, wg_lo_ref, wu_lo_ref, wd_lo_ref)
    refresh(1, ehi_ref, wg_hi_ref, wu_hi_ref, wd_hi_ref)

    @pl.when(i < nblk)
    def _():
        wait_gather(slot)
        x = xbuf[slot]
        xb = x.astype(BF16)

        def ffn(which):
            a = jnp.dot(xb, wgu_c[which], preferred_element_type=F32)
            hid = _silu(a[:, :d_expert]) * a[:, d_expert:]
            return jnp.dot(hid.astype(BF16), wd_c[which], preferred_element_type=F32)

        y = ffn(0) * gate_ref[:, 0:1] + ffn(1) * gate_ref[:, 1:2]
        res = _layer_norm_rows(alpha * x + y, lng_ref[...], lnb_ref[...])

        @pl.when(i >= 2)
        def _():
            wait_scatter(slot)

        ybuf[slot] = res
        for r in range(blk):
            pltpu.make_async_copy(ybuf.at[slot, pl.ds(r, 1)],
                                  out_hbm.at[pl.ds(dst_ref[0, 0, r], 1)], ssem.at[slot]).start()

        @pl.when(i == nblk - 1)
        def _():
            wait_scatter(slot)

            @pl.when(i >= 1)
            def _():
                wait_scatter(1 - slot)


def _moe_ffn(h, n_tok, src_idx, dst_idx, gates_sorted, blk_lo, blk_hi, n_used, wg, wu, wd,
             layer, ln_g, ln_b, alpha):
    d = h.shape[1]
    d_expert = wd.shape[2]
    lo_blk = lambda i, lo, hi, nb: (layer, lo[i], 0, 0)
    hi_blk = lambda i, lo, hi, nb: (layer, hi[i], 0, 0)
    n_blocks = src_idx.shape[0] // MOE_BLOCK
    src = src_idx.reshape(n_blocks, 1, MOE_BLOCK)
    dst = dst_idx.reshape(n_blocks, 1, MOE_BLOCK)
    last = n_blocks - 1
    smem_blk = functools.partial(pl.BlockSpec, (1, 1, MOE_BLOCK), memory_space=pltpu.SMEM)
    grid_spec = pltpu.PrefetchScalarGridSpec(
        num_scalar_prefetch=3,
        grid=(n_blocks,),
        in_specs=[
            smem_blk(lambda i, lo, hi, nb: (i, 0, 0)),
            smem_blk(lambda i, lo, hi, nb: (jnp.minimum(i + 1, last), 0, 0)),
            smem_blk(lambda i, lo, hi, nb: (i, 0, 0)),
            pl.BlockSpec((MOE_BLOCK, 2), lambda i, lo, hi, nb: (i, 0)),
            pl.BlockSpec(memory_space=pl.ANY),
            pl.BlockSpec((1, 1, d, d_expert), lo_blk),
            pl.BlockSpec((1, 1, d, d_expert), lo_blk),
            pl.BlockSpec((1, 1, d_expert, d), lo_blk),
            pl.BlockSpec((1, 1, d, d_expert), hi_blk),
            pl.BlockSpec((1, 1, d, d_expert), hi_blk),
            pl.BlockSpec((1, 1, d_expert, d), hi_blk),
            pl.BlockSpec((1, d), lambda i, lo, hi, nb: (0, 0)),
            pl.BlockSpec((1, d), lambda i, lo, hi, nb: (0, 0)),
        ],
        out_specs=pl.BlockSpec(memory_space=pl.ANY),
        scratch_shapes=[
            pltpu.VMEM((2, MOE_BLOCK, d), F32),
            pltpu.VMEM((2, MOE_BLOCK, d), F32),
            pltpu.VMEM((2, d, 2 * d_expert), BF16),
            pltpu.VMEM((2, d_expert, d), BF16),
            pltpu.SemaphoreType.DMA((2,)),
            pltpu.SemaphoreType.DMA((2,)),
        ],
    )
    return pl.pallas_call(
        functools.partial(_moe_kernel, alpha=alpha, d_expert=d_expert, n_tok=n_tok),
        grid_spec=grid_spec,
        out_shape=jax.ShapeDtypeStruct((n_tok + MOE_SPARE_ROWS, d), F32),
        compiler_params=_cparams(("arbitrary",)),
        name="moe_ffn",
    )(blk_lo, blk_hi, n_used, src, src, dst, gates_sorted, h, wg, wu, wd, wg, wu, wd,
      ln_g.reshape(1, d), ln_b.reshape(1, d))


def _dispatch_plan(route, n_tok):
    e1 = route[0].astype(jnp.int32)
    e2 = route[1].astype(jnp.int32)
    first_is_lo = e1 < e2
    lo = jnp.where(first_is_lo, e1, e2)
    hi = jnp.where(first_is_lo, e2, e1)
    g_lo = jnp.where(first_is_lo, route[2], route[3])
    g_hi = jnp.where(first_is_lo, route[3], route[2])
    grp = lo // EXPERTS_PER_GROUP
    a = lo % EXPERTS_PER_GROUP
    b = hi % EXPERTS_PER_GROUP
    cls = grp * PAIRS_PER_GROUP + (a * (7 - a)) // 2 + (b - a - 1)
    n_cls = N_GROUPS * PAIRS_PER_GROUP
    n_blocks = -(-(n_tok + n_cls * (MOE_BLOCK - 1)) // MOE_BLOCK)
    n_slots = n_blocks * MOE_BLOCK

    counts = jnp.bincount(cls, length=n_cls)
    start = jnp.cumsum(counts) - counts
    padded = (counts + MOE_BLOCK - 1) // MOE_BLOCK * MOE_BLOCK
    pend = jnp.cumsum(padded)
    pstart = pend - padded
    order = jnp.argsort(cls, stable=True).astype(jnp.int32)
    slot_ids = jnp.arange(n_slots)
    slot_cls = jnp.minimum(jnp.sum(slot_ids[:, None] >= pend[None, :], axis=1), n_cls - 1)
    rank = slot_ids - pstart[slot_cls]
    valid = rank < counts[slot_cls]
    slot_tok = jnp.where(valid, order[jnp.clip(start[slot_cls] + rank, 0, n_tok - 1)], 0)
    slot_tok = slot_tok.astype(jnp.int32)
    spare = n_tok + (slot_ids // MOE_BLOCK % 2) * MOE_BLOCK + slot_ids % MOE_BLOCK
    slot_dst = jnp.where(valid, slot_tok, spare).astype(jnp.int32)
    blk_cls = slot_cls[::MOE_BLOCK]
    pair_lo = jnp.array([0, 0, 0, 1, 1, 2], jnp.int32)
    pair_hi = jnp.array([1, 2, 3, 2, 3, 3], jnp.int32)
    blk_grp = blk_cls // PAIRS_PER_GROUP
    blk_pair = blk_cls % PAIRS_PER_GROUP
    blk_lo = (blk_grp * EXPERTS_PER_GROUP + pair_lo[blk_pair]).astype(jnp.int32)
    blk_hi = (blk_grp * EXPERTS_PER_GROUP + pair_hi[blk_pair]).astype(jnp.int32)
    n_used = (pend[-1] // MOE_BLOCK).astype(jnp.int32).reshape(1)
    gates_sorted = jnp.stack([g_lo[slot_tok], g_hi[slot_tok]], axis=1)
    return slot_tok, slot_dst, gates_sorted, blk_lo, blk_hi, n_used


def _moe_layer(h, route, wg, wu, wd, layer, ln_g, ln_b, alpha):
    n_tok = h.shape[0]
    slot_tok, slot_dst, gates_sorted, blk_lo, blk_hi, n_used = _dispatch_plan(route, n_tok)
    return _moe_ffn(h, n_tok, slot_tok, slot_dst, gates_sorted, blk_lo, blk_hi, n_used,
                    wg, wu, wd, layer, ln_g, ln_b, alpha)


def _pad_cols(w, width):
    return jnp.pad(w, ((0, 0), (0, width - w.shape[1])))


def kernel(x, ln_mix_g, ln_mix_b, ln_ffn_g, ln_ffn_b, router_w, router_b, fox_w_in, fox_b_f,
           fox_q_gain, fox_k_gain, fox_w_out, gdn_w_in, gdn_conv_w, gdn_a_log, gdn_dt_bias,
           gdn_norm_g, gdn_w_out, moe_w_gate, moe_w_up, moe_w_down):
    bsz, seq, d = x.shape
    n = bsz * seq
    depth = ln_mix_g.shape[0]
    alpha = (2.0 * depth) ** 0.25
    fox_heads = fox_b_f.shape[1]
    fox_dim = d // fox_heads
    gdn_heads = gdn_a_log.shape[1]
    gdn_dim = d // gdn_heads
    rw_t = router_w.T

    h = x.reshape(n, d)
    for i in range(depth):
        j = i // 2
        if i % 2 == 0:
            w_in = fox_w_in[j]
            gains = jnp.concatenate([jnp.tile(fox_q_gain[j], fox_heads) * (fox_dim ** -0.5 * LOG2E),
                                     jnp.tile(fox_k_gain[j], fox_heads)]).reshape(1, 2 * d)
            qk3, v_t, og, side = _inproj_fox(h, bsz, seq, w_in[:, :4 * d].astype(BF16),
                                             _pad_cols(w_in[:, 4 * d:], LANES).astype(BF16),
                                             gains, fox_dim)
            f_t = side[:, :fox_heads].reshape(bsz, seq, fox_heads).transpose(0, 2, 1)
            c = _fox_gates(f_t, fox_b_f[j])
            c_row = c.reshape(bsz, fox_heads // 2, 2, seq)
            o = _fox_attention(qk3, v_t, c_row, fox_heads, fox_dim)
            h, route = _mix_out(o.reshape(n, d), og, h, fox_w_out[j].astype(BF16),
                                ln_mix_g[i], ln_mix_b[i], rw_t, router_b, alpha)
        else:
            w_in = gdn_w_in[j]
            qkv, og, side = _inproj_gdn(h, bsz, seq, w_in[:, :4 * d].astype(BF16),
                                        _pad_cols(w_in[:, 4 * d:], LANES).astype(BF16),
                                        gdn_conv_w[j], gdn_dim)
            side3 = side.reshape(bsz, seq, LANES)
            b_t = side3[:, :, :gdn_heads].transpose(0, 2, 1)
            a_t = side3[:, :, gdn_heads:2 * gdn_heads].transpose(0, 2, 1)
            beta, gc = _gdn_gates(b_t, a_t, gdn_a_log[j], gdn_dt_bias[j])
            rows = jnp.concatenate([beta, gc], axis=1).reshape(
                bsz, 2 * gdn_heads, seq // CHUNK, CHUNK).transpose(0, 2, 1, 3)
            cols = rows.transpose(0, 1, 3, 2)
            o = _gdn_delta_rule(qkv, cols, rows, gdn_heads, gdn_dim)
            h, route = _mix_out(o.reshape(n, d), og, h, gdn_w_out[j].astype(BF16),
                                ln_mix_g[i], ln_mix_b[i], rw_t, router_b, alpha,
                                norm_g=gdn_norm_g[j], gdn_head_dim=gdn_dim)
        h = _moe_layer(h, route, moe_w_gate, moe_w_up, moe_w_down, i,
                       ln_ffn_g[i], ln_ffn_b[i], alpha)
    return h[:n].reshape(bsz, seq, d)
```
